```python
import jax, jax.numpy as jnp
from jax import lax
import numpy as np

D_MODEL = 1024
BATCH = 4
SEQ = 8192
DEPTH = 1

HEAD_DIM = 64
A_Q_HEADS = 8
A_KV_HEADS = 2
A_HALF_WINDOW = 128
B_HEADS = 8
B_PATTERNS = ((128, 1), (512, 4), (2048, 16))
D_FF = 2816
ROPE_THETA = 10000.0
NORM_EPS = 1e-6
FFN_RES_WEIGHT = 0.5

A_Q_W = A_Q_HEADS * HEAD_DIM
A_KV_W = A_KV_HEADS * HEAD_DIM
B_W = B_HEADS * HEAD_DIM
IN_W = A_Q_W + 2 * A_KV_W + 3 * B_W
MIX_W = A_Q_W + B_W

kernel_name = "hybrid_window_gqa_dilated_macaron_encoder"


def rms_norm(x, g):
    xf = x.astype(jnp.float32)
    y = xf * lax.rsqrt(jnp.mean(xf * xf, axis=-1, keepdims=True) + NORM_EPS)
    return (y * g.astype(jnp.float32)).astype(x.dtype)


def swiglu(h, w_gate, w_up, w_down):
    return (jax.nn.silu(h @ w_gate) * (h @ w_up)) @ w_down


def rope_tables(positions):
    inv_freq = 1.0 / (ROPE_THETA ** (jnp.arange(0, HEAD_DIM, 2, dtype=jnp.float32) / HEAD_DIM))
    ang = positions.astype(jnp.float32)[..., None] * inv_freq
    return jnp.cos(ang)[:, :, None, :], jnp.sin(ang)[:, :, None, :]


def apply_rope(t, cos, sin):
    tf = t.astype(jnp.float32)
    t1, t2 = jnp.split(tf, 2, axis=-1)
    return jnp.concatenate([t1 * cos - t2 * sin, t2 * cos + t1 * sin], axis=-1).astype(t.dtype)


def banded_attention(q, k, v, half_window, sink=None):
    blk = half_window
    B, L, Hq, Dh = q.shape
    Hkv = k.shape[2]
    G = Hq // Hkv
    nb = -(-L // blk)
    Lp = nb * blk
    pad = Lp - L
    qb = jnp.pad(q, ((0, 0), (0, pad), (0, 0), (0, 0))).astype(jnp.float32).reshape(B, nb, blk, Hkv, G, Dh)
    kp = jnp.pad(k, ((0, 0), (blk, blk + pad), (0, 0), (0, 0))).astype(jnp.float32)
    vp = jnp.pad(v, ((0, 0), (blk, blk + pad), (0, 0), (0, 0))).astype(jnp.float32)
    kw = jnp.concatenate([kp[:, j * blk:j * blk + Lp].reshape(B, nb, blk, Hkv, Dh) for j in range(3)], axis=2)
    vw = jnp.concatenate([vp[:, j * blk:j * blk + Lp].reshape(B, nb, blk, Hkv, Dh) for j in range(3)], axis=2)
    qpos = jnp.arange(Lp).reshape(nb, blk)
    kpos = jnp.arange(nb)[:, None] * blk + jnp.arange(3 * blk)[None, :] - blk
    valid = (jnp.abs(qpos[:, :, None] - kpos[:, None, :]) <= half_window) & (kpos[:, None, :] >= 0) & (kpos[:, None, :] < L)
    s = jnp.einsum('bnqhgd,bnkhd->bnhgqk', qb, kw) * (Dh ** -0.5)
    s = jnp.where(valid[None, :, None, None], s, -jnp.inf)
    m = jnp.max(s, axis=-1)
    if sink is not None:
        sk = sink.astype(jnp.float32).reshape(Hkv, G)[None, None, :, :, None]
        m = jnp.maximum(m, sk)
    p = jnp.exp(s - m[..., None])
    den = jnp.sum(p, axis=-1)
    if sink is not None:
        den = den + jnp.exp(sk - m)
    o = jnp.einsum('bnhgqk,bnkhd->bnqhgd', p, vw) / jnp.transpose(den, (0, 1, 4, 2, 3))[..., None]
    lse = jnp.transpose(m + jnp.log(den), (0, 1, 4, 2, 3)).reshape(B, Lp, Hq)[:, :L]
    return o.reshape(B, Lp, Hq, Dh)[:, :L], lse


def dilated_window_attention(q, k, v, window, dilation):
    B, S, H, Dh = q.shape
    msub = S // dilation

    def to_sub(t):
        return t.reshape(B, msub, dilation, H, Dh).transpose(0, 2, 1, 3, 4).reshape(B * dilation, msub, H, Dh)

    o, lse = banded_attention(to_sub(q), to_sub(k), to_sub(v), window // (2 * dilation))
    o = o.reshape(B, dilation, msub, H, Dh).transpose(0, 2, 1, 3, 4).reshape(B, S, H, Dh)
    lse = lse.reshape(B, dilation, msub, H).transpose(0, 2, 1, 3).reshape(B, S, H)
    return o, lse


def mixer(h, w_in, a_sink, w_out, cos, sin):
    B, S, _ = h.shape
    proj = h @ w_in
    cuts = np.cumsum([A_Q_W, A_KV_W, A_KV_W, B_W, B_W]).tolist()
    aq, ak, av, bq, bk, bv = jnp.split(proj, cuts, axis=-1)
    aq = apply_rope(aq.reshape(B, S, A_Q_HEADS, HEAD_DIM), cos, sin)
    ak = apply_rope(ak.reshape(B, S, A_KV_HEADS, HEAD_DIM), cos, sin)
    av = av.reshape(B, S, A_KV_HEADS, HEAD_DIM)
    bq = apply_rope(bq.reshape(B, S, B_HEADS, HEAD_DIM), cos, sin)
    bk = apply_rope(bk.reshape(B, S, B_HEADS, HEAD_DIM), cos, sin)
    bv = bv.reshape(B, S, B_HEADS, HEAD_DIM)
    a_out, _ = banded_attention(aq, ak, av, A_HALF_WINDOW, sink=a_sink)
    outs, lses = [], []
    for w, d in B_PATTERNS:
        o, l = dilated_window_attention(bq, bk, bv, w, d)
        outs.append(o)
        lses.append(l)
    wts = jax.nn.softmax(jnp.stack(lses, axis=0), axis=0)
    b_out = jnp.sum(wts[..., None] * jnp.stack(outs, axis=0), axis=0)
    cat = jnp.concatenate([a_out.reshape(B, S, A_Q_W), b_out.reshape(B, S, B_W)], axis=-1).astype(h.dtype)
    return cat @ w_out


def setup_inputs(seed: int = 0) -> dict:
    key = jax.random.key(seed)
    ks = jax.random.split(key, 16)
    f32 = jnp.float32
    nrm = lambda k, shape, scale: jax.random.normal(k, shape, f32) * scale
    gain = lambda k: 1.0 + 0.02 * jax.random.normal(k, (DEPTH, D_MODEL), f32)
    x = jax.random.normal(ks[0], (BATCH, SEQ, D_MODEL), f32)
    offsets = jax.random.randint(ks[1], (BATCH, 1), 0, 4096, dtype=jnp.int32)
    positions = (jnp.arange(SEQ, dtype=jnp.int32)[None, :] + offsets).astype(jnp.int32)
    return {
        "x": x,
        "positions": positions,
        "norm_ffn1": gain(ks[2]),
        "w_gate1": nrm(ks[3], (DEPTH, D_MODEL, D_FF), D_MODEL ** -0.5),
        "w_up1": nrm(ks[4], (DEPTH, D_MODEL, D_FF), D_MODEL ** -0.5),
        "w_down1": nrm(ks[5], (DEPTH, D_FF, D_MODEL), D_FF ** -0.5),
        "norm_mix": gain(ks[6]),
        "w_in": nrm(ks[7], (DEPTH, D_MODEL, IN_W), D_MODEL ** -0.5),
        "a_sink": nrm(ks[8], (DEPTH, A_Q_HEADS), 0.5),
        "w_out": nrm(ks[9], (DEPTH, MIX_W, D_MODEL), MIX_W ** -0.5),
        "norm_ffn2": gain(ks[10]),
        "w_gate2": nrm(ks[11], (DEPTH, D_MODEL, D_FF), D_MODEL ** -0.5),
        "w_up2": nrm(ks[12], (DEPTH, D_MODEL, D_FF), D_MODEL ** -0.5),
        "w_down2": nrm(ks[13], (DEPTH, D_FF, D_MODEL), D_FF ** -0.5),
        "norm_final": 1.0 + 0.02 * jax.random.normal(ks[14], (D_MODEL,), f32),
    }


def reference(x, positions, norm_ffn1, w_gate1, w_up1, w_down1, norm_mix, w_in, a_sink, w_out,
              norm_ffn2, w_gate2, w_up2, w_down2, norm_final):
    cos, sin = rope_tables(positions)
    for l in range(DEPTH):
        x = x + FFN_RES_WEIGHT * swiglu(rms_norm(x, norm_ffn1[l]), w_gate1[l], w_up1[l], w_down1[l])
        x = x + mixer(rms_norm(x, norm_mix[l]), w_in[l], a_sink[l], w_out[l], cos, sin)
        x = x + FFN_RES_WEIGHT * swiglu(rms_norm(x, norm_ffn2[l]), w_gate2[l], w_up2[l], w_down2[l])
    return rms_norm(x, norm_final)
```

```python
import functools

import jax
import jax.numpy as jnp
import numpy as np
from jax import lax
from jax.experimental import pallas as pl
from jax.experimental.pallas import tpu as pltpu

HEAD_DIM = 64
A_Q_HEADS = 8
A_KV_HEADS = 2
A_HALF_WINDOW = 128
B_HEADS = 8
B_PATTERNS = ((128, 1), (512, 4), (2048, 16))
ROPE_THETA = 10000.0
NORM_EPS = 1e-6
FFN_RES_WEIGHT = 0.5

A_Q_W = A_Q_HEADS * HEAD_DIM
A_KV_W = A_KV_HEADS * HEAD_DIM
B_W = B_HEADS * HEAD_DIM

LANES = 128
NEG_BIG = -1e30
TOKEN_TILE = 512
FF_CHUNK = 256
ATTN_Q_TILE = 256
LSE_LANES = LANES // B_HEADS
VMEM_LIMIT = 56 * 1024 * 1024


def _rms_norm(x, g):
    y = x * lax.rsqrt(jnp.mean(x * x, axis=-1, keepdims=True) + NORM_EPS)
    return y * g


def _swiglu_accumulate(h_bf16, wgu_ref, wd_ref, acc_ref):
    n_chunks = wgu_ref.shape[0]
    fc = wd_ref.shape[1]
    acc_ref[...] = jnp.zeros_like(acc_ref)

    def body(c, carry):
        gu = jnp.dot(h_bf16, wgu_ref[c], preferred_element_type=jnp.float32)
        g = gu[:, :fc]
        u = gu[:, fc:]
        a = (g * jax.nn.sigmoid(g) * u).astype(jnp.bfloat16)
        acc_ref[...] += jnp.dot(a, wd_ref[c], preferred_element_type=jnp.float32)
        return carry

    lax.fori_loop(0, n_chunks, body, 0)


def _rope(t, cos, sin_signed, first_half):
    parts = []
    for j in range(t.shape[1] // LANES):
        tj = t[:, j * LANES:(j + 1) * LANES]
        partner = jnp.where(first_half, pltpu.roll(tj, LANES - HEAD_DIM // 2, 1), pltpu.roll(tj, HEAD_DIM // 2, 1))
        parts.append(tj * cos + partner * sin_signed)
    return parts[0] if len(parts) == 1 else jnp.concatenate(parts, axis=1)


def _ffn_in_proj_kernel(x_ref, pos_ref, freq_ref, g1_ref, wgu_ref, wd_ref, gm_ref, win_ref,
                        x1_ref, aq_ref, ak_ref, av_ref, bq_ref, bk_ref, bv_ref, acc_ref):
    x = x_ref[...]
    h = _rms_norm(x, g1_ref[...]).astype(jnp.bfloat16)
    _swiglu_accumulate(h, wgu_ref, wd_ref, acc_ref)
    x1 = x + FFN_RES_WEIGHT * acc_ref[...]
    x1_ref[...] = x1
    h2 = _rms_norm(x1, gm_ref[...]).astype(jnp.bfloat16)

    ang = pos_ref[...].astype(jnp.float32) * freq_ref[...]
    lane = lax.broadcasted_iota(jnp.int32, ang.shape, 1)
    first_half = (lane % HEAD_DIM) < (HEAD_DIM // 2)
    cos = jnp.cos(ang)
    sin = jnp.sin(ang)
    sin_signed = jnp.where(first_half, -sin, sin)
    scale = HEAD_DIM ** -0.5

    def proj(lo, width):
        return jnp.dot(h2, win_ref[:, lo:lo + width], preferred_element_type=jnp.float32)

    o = 0
    aq_ref[...] = (_rope(proj(o, A_Q_W), cos, sin_signed, first_half) * scale).astype(aq_ref.dtype)
    o += A_Q_W
    ak_ref[...] = _rope(proj(o, A_KV_W), cos, sin_signed, first_half).astype(ak_ref.dtype)
    o += A_KV_W
    av_ref[...] = proj(o, A_KV_W).astype(av_ref.dtype)
    o += A_KV_W
    bq_ref[...] = (_rope(proj(o, B_W), cos, sin_signed, first_half) * scale).astype(bq_ref.dtype)
    o += B_W
    bk_ref[...] = _rope(proj(o, B_W), cos, sin_signed, first_half).astype(bk_ref.dtype)
    o += B_W
    bv_ref[...] = proj(o, B_W).astype(bv_ref.dtype)


def _out_ffn_kernel(x1_ref, a_ref, b_ref, woa_ref, wob_ref, g2_ref, wgu_ref, wd_ref, gf_ref,
                    out_ref, acc_ref, *, final_norm):
    mix = jnp.dot(a_ref[...], woa_ref[...], preferred_element_type=jnp.float32)
    mix += jnp.dot(b_ref[...], wob_ref[...], preferred_element_type=jnp.float32)
    x2 = x1_ref[...] + mix
    h = _rms_norm(x2, g2_ref[...]).astype(jnp.bfloat16)
    _swiglu_accumulate(h, wgu_ref, wd_ref, acc_ref)
    x3 = x2 + FFN_RES_WEIGHT * acc_ref[...]
    out_ref[...] = _rms_norm(x3, gf_ref[...]) if final_norm else x3


def _band_bias(tile_idx, tq, halo, seq_len):
    tk = tq + 2 * halo
    row = lax.broadcasted_iota(jnp.int32, (tq, tk), 0)
    col = lax.broadcasted_iota(jnp.int32, (tq, tk), 1)
    kpos = tile_idx * tq - halo + col
    diff = row + halo - col
    valid = (jnp.abs(diff) <= halo) & (kpos >= 0) & (kpos < seq_len)
    return jnp.where(valid, 0.0, NEG_BIG).astype(jnp.float32)


def _scores(qh, kh):
    return lax.dot_general(qh, kh, (((1,), (1,)), ((), ())), preferred_element_type=jnp.float32)


def _attn_a_kernel(sink_ref, q_ref, kp_ref, kc_ref, kn_ref, vp_ref, vc_ref, vn_ref, o_ref, *, seq_len):
    tq = q_ref.shape[0]
    bias = _band_bias(pl.program_id(1), tq, A_HALF_WINDOW, seq_len)
    q = q_ref[...]
    k = jnp.concatenate([kp_ref[...], kc_ref[...], kn_ref[...]], axis=0)
    v = jnp.concatenate([vp_ref[...], vc_ref[...], vn_ref[...]], axis=0)
    group = A_Q_HEADS // A_KV_HEADS
    outs = []
    for h in range(A_Q_HEADS):
        g = h // group
        qh = q[:, h * HEAD_DIM:(h + 1) * HEAD_DIM]
        kh = k[:, g * HEAD_DIM:(g + 1) * HEAD_DIM]
        vh = v[:, g * HEAD_DIM:(g + 1) * HEAD_DIM]
        s = _scores(qh, kh) + bias
        sink = sink_ref[h]
        m = jnp.maximum(jnp.max(s, axis=-1, keepdims=True), sink)
        p = jnp.exp(s - m)
        den = jnp.sum(p, axis=-1, keepdims=True) + jnp.exp(sink - m)
        o = jnp.dot(p.astype(jnp.bfloat16), vh, preferred_element_type=jnp.float32)
        outs.append(o / den)
    o_ref[...] = jnp.concatenate(outs, axis=1).astype(o_ref.dtype)


def _attn_b_kernel(*refs, sub_len, halo, has_prev, is_last):
    q_ref, kp_ref, kc_ref, kn_ref, vp_ref, vc_ref, vn_ref = refs[:7]
    refs = refs[7:]
    if has_prev:
        o_prev_ref, lse_prev_ref = refs[:2]
        refs = refs[2:]
    o_ref = refs[0]
    lse_ref = None if is_last else refs[1]

    tq = q_ref.shape[0]
    bias = _band_bias(pl.program_id(2), tq, halo, sub_len)
    q = q_ref[...]
    k = jnp.concatenate([kp_ref[...], kc_ref[...], kn_ref[...]], axis=0)
    v = jnp.concatenate([vp_ref[...], vc_ref[...], vn_ref[...]], axis=0)
    lse_lane_head = lax.broadcasted_iota(jnp.int32, (tq, LANES), 1) // LSE_LANES
    if has_prev:
        lse_prev = lse_prev_ref[...]
    outs = []
    lse_out = jnp.zeros((tq, LANES), jnp.float32)
    for h in range(B_HEADS):
        sl = slice(h * HEAD_DIM, (h + 1) * HEAD_DIM)
        s = _scores(q[:, sl], k[:, sl]) + bias
        m = jnp.max(s, axis=-1, keepdims=True)
        p = jnp.exp(s - m)
        den = jnp.sum(p, axis=-1, keepdims=True)
        o = jnp.dot(p.astype(jnp.bfloat16), v[:, sl], preferred_element_type=jnp.float32) / den
        lse = m + jnp.log(den)
        if has_prev:
            lse_p = jnp.max(jnp.where(lse_lane_head == h, lse_prev, NEG_BIG), axis=-1, keepdims=True)
            mm = jnp.maximum(lse, lse_p)
            wa = jnp.exp(lse - mm)
            wb = jnp.exp(lse_p - mm)
            tot = wa + wb
            o = (wa * o + wb * o_prev_ref[:, sl]) / tot
            lse = mm + jnp.log(tot)
        outs.append(o)
        if not is_last:
            lse_out = jnp.where(lse_lane_head == h, lse, lse_out)
    o_ref[...] = jnp.concatenate(outs, axis=1).astype(o_ref.dtype)
    if not is_last:
        lse_ref[...] = lse_out


def _const_spec(shape):
    zeros = (0,) * len(shape)
    return pl.BlockSpec(shape, lambda *_: zeros, pipeline_mode=pl.Buffered(1))


def _stack_ffn_weights(w_gate, w_up, w_down):
    d, ff = w_gate.shape
    nc = ff // FF_CHUNK
    wg = w_gate.reshape(d, nc, FF_CHUNK).transpose(1, 0, 2)
    wu = w_up.reshape(d, nc, FF_CHUNK).transpose(1, 0, 2)
    wgu = jnp.concatenate([wg, wu], axis=-1).astype(jnp.bfloat16)
    wd = w_down.reshape(nc, FF_CHUNK, d).astype(jnp.bfloat16)
    return wgu, wd


def _ffn_in_proj(x2d, pos2d, freq, g1, wgu, wd, gm, w_in):
    n, d = x2d.shape
    tm = TOKEN_TILE
    row = lambda w: pl.BlockSpec((tm, w), lambda i: (i, 0))
    bf = jnp.bfloat16
    out_shape = (
        jax.ShapeDtypeStruct((n, d), jnp.float32),
        jax.ShapeDtypeStruct((n, A_Q_W), bf), jax.ShapeDtypeStruct((n, A_KV_W), bf),
        jax.ShapeDtypeStruct((n, A_KV_W), bf), jax.ShapeDtypeStruct((n, B_W), bf),
        jax.ShapeDtypeStruct((n, B_W), bf), jax.ShapeDtypeStruct((n, B_W), bf),
    )
    return pl.pallas_call(
        _ffn_in_proj_kernel,
        grid=(n // tm,),
        in_specs=[row(d), row(1), _const_spec(freq.shape), _const_spec(g1.shape), _const_spec(wgu.shape),
                  _const_spec(wd.shape), _const_spec(gm.shape), _const_spec(w_in.shape)],
        out_specs=(row(d), row(A_Q_W), row(A_KV_W), row(A_KV_W), row(B_W), row(B_W), row(B_W)),
        out_shape=out_shape,
        scratch_shapes=[pltpu.VMEM((tm, d), jnp.float32)],
        compiler_params=pltpu.CompilerParams(dimension_semantics=("arbitrary",), vmem_limit_bytes=VMEM_LIMIT),
        name="ffn_in_proj",
    )(x2d, pos2d, freq, g1, wgu, wd, gm, w_in)


def _out_ffn(x1, a, b, woa, wob, g2, wgu, wd, gf, final_norm):
    n, d = x1.shape
    tm = TOKEN_TILE
    row = lambda w: pl.BlockSpec((tm, w), lambda i: (i, 0))
    return pl.pallas_call(
        functools.partial(_out_ffn_kernel, final_norm=final_norm),
        grid=(n // tm,),
        in_specs=[row(d), row(A_Q_W), row(B_W), _const_spec(woa.shape), _const_spec(wob.shape),
                  _const_spec(g2.shape), _const_spec(wgu.shape), _const_spec(wd.shape), _const_spec(gf.shape)],
        out_specs=row(d),
        out_shape=jax.ShapeDtypeStruct((n, d), jnp.float32),
        scratch_shapes=[pltpu.VMEM((tm, d), jnp.float32)],
        compiler_params=pltpu.CompilerParams(dimension_semantics=("arbitrary",), vmem_limit_bytes=VMEM_LIMIT),
        name="out_ffn",
    )(x1, a, b, woa, wob, g2, wgu, wd, gf)


def _halo_specs(tq, halo, width, tiles_per_seq, row_block_of, col_block_of):
    ratio = tq // halo
    halo_blocks = tiles_per_seq * ratio

    def prev_map(*idx):
        i = idx[-1]
        return (row_block_of(idx, halo_blocks, jnp.maximum(i * ratio - 1, 0)), col_block_of(idx))

    def cur_map(*idx):
        return (row_block_of(idx, tiles_per_seq, idx[-1]), col_block_of(idx))

    def next_map(*idx):
        i = idx[-1]
        return (row_block_of(idx, halo_blocks, jnp.minimum((i + 1) * ratio, halo_blocks - 1)), col_block_of(idx))

    return (pl.BlockSpec((halo, width), prev_map), pl.BlockSpec((tq, width), cur_map),
            pl.BlockSpec((halo, width), next_map))


def _attn_a(sink, aq, ak, av, batch, seq_len):
    n = aq.shape[0]
    tq = ATTN_Q_TILE
    tiles = seq_len // tq
    row_block_of = lambda idx, blocks_per_seq, j: idx[0] * blocks_per_seq + j
    col0 = lambda idx: 0
    kv_specs = _halo_specs(tq, A_HALF_WINDOW, A_KV_W, tiles, row_block_of, col0)
    q_spec = pl.BlockSpec((tq, A_Q_W), lambda b, i: (b * tiles + i, 0))
    return pl.pallas_call(
        functools.partial(_attn_a_kernel, seq_len=seq_len),
        grid=(batch, tiles),
        in_specs=[pl.BlockSpec(memory_space=pltpu.SMEM), q_spec, *kv_specs, *kv_specs],
        out_specs=q_spec,
        out_shape=jax.ShapeDtypeStruct((n, A_Q_W), jnp.bfloat16),
        compiler_params=pltpu.CompilerParams(dimension_semantics=("arbitrary", "arbitrary")),
        name="attn_a",
    )(sink, aq, ak, ak, ak, av, av, av)


def _attn_b_pattern(bq, bk, bv, prev, batch, seq_len, window, dilation, is_last):
    n = bq.shape[0]
    sub_len = seq_len // dilation
    halo = window // (2 * dilation)
    tq = ATTN_Q_TILE
    tiles = sub_len // tq
    view = lambda t: t.reshape(n // dilation, dilation * t.shape[1])
    row_block_of = lambda idx, blocks_per_seq, j: idx[0] * blocks_per_seq + j
    col_r = lambda idx: idx[1]
    kv_specs = _halo_specs(tq, halo, B_W, tiles, row_block_of, col_r)
    tile_spec = lambda w: pl.BlockSpec((tq, w), lambda b, r, i: (b * tiles + i, r))
    in_specs = [tile_spec(B_W), *kv_specs, *kv_specs]
    args = [view(bq), view(bk), view(bk), view(bk), view(bv), view(bv), view(bv)]
    if prev is not None:
        in_specs += [tile_spec(B_W), tile_spec(LANES)]
        args += [view(prev[0]), view(prev[1])]
    if is_last:
        out_specs = tile_spec(B_W)
        out_shape = jax.ShapeDtypeStruct((n // dilation, dilation * B_W), jnp.bfloat16)
    else:
        out_specs = (tile_spec(B_W), tile_spec(LANES))
        out_shape = (jax.ShapeDtypeStruct((n // dilation, dilation * B_W), jnp.float32),
                     jax.ShapeDtypeStruct((n // dilation, dilation * LANES), jnp.float32))
    res = pl.pallas_call(
        functools.partial(_attn_b_kernel, sub_len=sub_len, halo=halo, has_prev=prev is not None, is_last=is_last),
        grid=(batch, dilation, tiles),
        in_specs=in_specs,
        out_specs=out_specs,
        out_shape=out_shape,
        compiler_params=pltpu.CompilerParams(dimension_semantics=("arbitrary",) * 3),
        name=f"attn_b_d{dilation}",
    )(*args)
    if is_last:
        return res.reshape(n, B_W)
    return res[0].reshape(n, B_W), res[1].reshape(n, LANES)


def kernel(x, positions, norm_ffn1, w_gate1, w_up1, w_down1, norm_mix, w_in, a_sink, w_out,
           norm_ffn2, w_gate2, w_up2, w_down2, norm_final):
    batch, seq_len, d = x.shape
    n = batch * seq_len
    depth = norm_ffn1.shape[0]
    bf = jnp.bfloat16

    inv_freq = 1.0 / (ROPE_THETA ** (jnp.arange(0, HEAD_DIM, 2, dtype=jnp.float32) / HEAD_DIM))
    freq = jnp.tile(inv_freq, LANES // (HEAD_DIM // 2))[None, :]
    pos2d = positions.reshape(n, 1)
    xc = x.reshape(n, d)

    for l in range(depth):
        wgu1, wd1 = _stack_ffn_weights(w_gate1[l], w_up1[l], w_down1[l])
        wgu2, wd2 = _stack_ffn_weights(w_gate2[l], w_up2[l], w_down2[l])
        x1, aq, ak, av, bq, bk, bv = _ffn_in_proj(
            xc, pos2d, freq, norm_ffn1[l][None, :], wgu1, wd1, norm_mix[l][None, :], w_in[l].astype(bf))
        a_out = _attn_a(a_sink[l], aq, ak, av, batch, seq_len)
        state = None
        for p, (window, dilation) in enumerate(B_PATTERNS):
            state = _attn_b_pattern(bq, bk, bv, state, batch, seq_len, window, dilation,
                                    is_last=p == len(B_PATTERNS) - 1)
        b_out = state
        w_out_l = w_out[l].astype(bf)
        xc = _out_ffn(x1, a_out, b_out, w_out_l[:A_Q_W], w_out_l[A_Q_W:], norm_ffn2[l][None, :],
                      wgu2, wd2, norm_final[None, :], final_norm=l == depth - 1)
    return xc.reshape(batch, seq_len, d)
```

```python
import functools

import jax
import jax.numpy as jnp
from jax import lax
from jax.experimental import pallas as pl
from jax.experimental.pallas import tpu as pltpu

HEAD_DIM = 64
A_Q_HEADS = 8
A_KV_HEADS = 2
A_HALF_WINDOW = 128
B_HEADS = 8
B_PATTERNS = ((128, 1), (512, 4), (2048, 16))
ROPE_THETA = 10000.0
NORM_EPS = 1e-6
FFN_RES_WEIGHT = 0.5

A_Q_W = A_Q_HEADS * HEAD_DIM
A_KV_W = A_KV_HEADS * HEAD_DIM
B_W = B_HEADS * HEAD_DIM

LANES = 128
NEG_BIG = -1e30
TOKEN_TILE = 512
FF_CHUNK = 256
ATTN_STEP_ROWS = 512
VMEM_LIMIT = 56 * 1024 * 1024
A_HEAD_ORDER = (0, 4, 1, 5, 2, 6, 3, 7)
DILATIONS = tuple(d for _, d in B_PATTERNS if d > 1)


def _rms_norm(x, g):
    y = x * lax.rsqrt(jnp.mean(x * x, axis=-1, keepdims=True) + NORM_EPS)
    return y * g


def _swiglu_accumulate(h_bf16, wgu_ref, wd_ref, acc_ref):
    n_chunks = wgu_ref.shape[0]
    fc = wd_ref.shape[1]
    acc_ref[...] = jnp.zeros_like(acc_ref)

    def body(c, carry):
        gu = jnp.dot(h_bf16, wgu_ref[c], preferred_element_type=jnp.float32)
        g = gu[:, :fc]
        u = gu[:, fc:]
        a = (g * jax.nn.sigmoid(g) * u).astype(jnp.bfloat16)
        acc_ref[...] += jnp.dot(a, wd_ref[c], preferred_element_type=jnp.float32)
        return carry

    lax.fori_loop(0, n_chunks, body, 0)


def _rope(t, cos, sin_signed, first_half):
    parts = []
    for j in range(t.shape[1] // LANES):
        tj = t[:, j * LANES:(j + 1) * LANES]
        partner = jnp.where(first_half, pltpu.roll(tj, LANES - HEAD_DIM // 2, 1), pltpu.roll(tj, HEAD_DIM // 2, 1))
        parts.append(tj * cos + partner * sin_signed)
    return parts[0] if len(parts) == 1 else jnp.concatenate(parts, axis=1)


def _write_deinterleaved(val, slab_ref, out_refs):
    tm, width = val.shape
    n_slabs = width // LANES
    for c in range(n_slabs):
        slab_ref[c] = val[:, c * LANES:(c + 1) * LANES]
    for d, out_ref in out_refs.items():
        for r in range(d):
            for c in range(n_slabs):
                rows = slab_ref[c, pl.ds(r, tm // d, stride=d), :]
                out_ref[r, :, c * LANES:(c + 1) * LANES] = rows.astype(out_ref.dtype)


def _ffn_in_proj_kernel(x_ref, pos_ref, freq_ref, g1_ref, wgu_ref, wd_ref, gm_ref, win_ref,
                        x1_ref, aq_ref, ak_ref, av_ref, bq_ref, bk_ref, bv_ref, *rest):
    n_dil = len(DILATIONS)
    bq_d = dict(zip(DILATIONS, rest[0:n_dil]))
    bk_d = dict(zip(DILATIONS, rest[n_dil:2 * n_dil]))
    bv_d = dict(zip(DILATIONS, rest[2 * n_dil:3 * n_dil]))
    acc_ref, slab_ref = rest[3 * n_dil:]

    x = x_ref[...]
    h = _rms_norm(x, g1_ref[...]).astype(jnp.bfloat16)
    _swiglu_accumulate(h, wgu_ref, wd_ref, acc_ref)
    x1 = x + FFN_RES_WEIGHT * acc_ref[...]
    x1_ref[...] = x1
    h2 = _rms_norm(x1, gm_ref[...]).astype(jnp.bfloat16)

    ang = pos_ref[...].astype(jnp.float32) * freq_ref[...]
    lane = lax.broadcasted_iota(jnp.int32, ang.shape, 1)
    first_half = (lane % HEAD_DIM) < (HEAD_DIM // 2)
    cos = jnp.cos(ang)
    sin = jnp.sin(ang)
    sin_signed = jnp.where(first_half, -sin, sin)
    scale = HEAD_DIM ** -0.5

    def proj(lo, width):
        return jnp.dot(h2, win_ref[:, lo:lo + width], preferred_element_type=jnp.float32)

    o = 0
    aq_ref[...] = (_rope(proj(o, A_Q_W), cos, sin_signed, first_half) * scale).astype(aq_ref.dtype)
    o += A_Q_W
    ak_ref[...] = _rope(proj(o, A_KV_W), cos, sin_signed, first_half).astype(ak_ref.dtype)
    o += A_KV_W
    av_ref[...] = proj(o, A_KV_W).astype(av_ref.dtype)
    o += A_KV_W
    bq = _rope(proj(o, B_W), cos, sin_signed, first_half) * scale
    bq_ref[...] = bq.astype(bq_ref.dtype)
    _write_deinterleaved(bq, slab_ref, bq_d)
    o += B_W
    bk = _rope(proj(o, B_W), cos, sin_signed, first_half)
    bk_ref[...] = bk.astype(bk_ref.dtype)
    _write_deinterleaved(bk, slab_ref, bk_d)
    o += B_W
    bv = proj(o, B_W)
    bv_ref[...] = bv.astype(bv_ref.dtype)
    _write_deinterleaved(bv, slab_ref, bv_d)


def _interleave_into(slab_ref, blk_ref, d):
    rows = blk_ref.shape[1]
    for r in range(d):
        for c in range(slab_ref.shape[0]):
            slab_ref[c, pl.ds(r, rows, stride=d), :] = blk_ref[r, :, c * LANES:(c + 1) * LANES].astype(jnp.float32)


def _out_ffn_kernel(x1_ref, a_ref, o1_ref, l1_ref, o4_ref, l4_ref, o16_ref, l16_ref,
                    woa_ref, wob_ref, g2_ref, wgu_ref, wd_ref, gf_ref,
                    out_ref, acc_ref, so4_ref, sl4_ref, so16_ref, sl16_ref, *, final_norm):
    d4, d16 = DILATIONS
    _interleave_into(so4_ref, o4_ref, d4)
    _interleave_into(sl4_ref, l4_ref, d4)
    _interleave_into(so16_ref, o16_ref, d16)
    _interleave_into(sl16_ref, l16_ref, d16)

    mix = jnp.dot(a_ref[...], woa_ref[...], preferred_element_type=jnp.float32)
    for c in range(B_W // LANES):
        sl = slice(c * LANES, (c + 1) * LANES)
        la, lb, lc = l1_ref[:, sl], sl4_ref[c], sl16_ref[c]
        m = jnp.maximum(jnp.maximum(la, lb), lc)
        wa, wb, wc = jnp.exp(la - m), jnp.exp(lb - m), jnp.exp(lc - m)
        num = wa * o1_ref[:, sl].astype(jnp.float32) + wb * so4_ref[c] + wc * so16_ref[c]
        b = (num / (wa + wb + wc)).astype(jnp.bfloat16)
        mix += jnp.dot(b, wob_ref[sl, :], preferred_element_type=jnp.float32)
    x2 = x1_ref[...] + mix
    h = _rms_norm(x2, g2_ref[...]).astype(jnp.bfloat16)
    _swiglu_accumulate(h, wgu_ref, wd_ref, acc_ref)
    x3 = x2 + FFN_RES_WEIGHT * acc_ref[...]
    out_ref[...] = _rms_norm(x3, gf_ref[...]) if final_norm else x3


def _attn_kernel(*refs, seq_len, tq, halo, sinks, emit_lse):
    if sinks:
        sink_ref, refs = refs[0], refs[1:]
    q_ref, kp_ref, kc_ref, kn_ref, vp_ref, vc_ref, vn_ref, o_ref = refs[:8]
    refs = refs[8:]
    if emit_lse:
        lse_ref, refs = refs[0], refs[1:]
    kbuf, vbuf, bias_ref = refs

    rows = q_ref.shape[0]
    tk = tq + 2 * halo
    tiles = rows // tq
    tiles_in_seq = seq_len // tq
    kv_tiles = kbuf.shape[1] // LANES
    q_tiles = q_ref.shape[1] // LANES
    step = pl.program_id(2)

    @pl.when((pl.program_id(0) == 0) & (pl.program_id(1) == 0) & (step == 0))
    def _():
        row = lax.broadcasted_iota(jnp.int32, (tq, tk), 0)
        col = lax.broadcasted_iota(jnp.int32, (tq, tk), 1)
        band = jnp.abs(row + halo - col) <= halo
        bias_ref[0] = jnp.where(band & (col >= halo), 0.0, NEG_BIG)
        bias_ref[1] = jnp.where(band, 0.0, NEG_BIG)
        bias_ref[2] = jnp.where(band & (col < tq + halo), 0.0, NEG_BIG)

    kbuf[0:halo, :] = kp_ref[...]
    kbuf[halo:halo + rows, :] = kc_ref[...]
    kbuf[halo + rows:, :] = kn_ref[...]
    vbuf[0:halo, :] = vp_ref[...]
    vbuf[halo:halo + rows, :] = vc_ref[...]
    vbuf[halo + rows:, :] = vn_ref[...]

    lane = lax.broadcasted_iota(jnp.int32, (tq, LANES), 1)
    low = lane < HEAD_DIM
    ones = jnp.ones((tk, LANES), jnp.bfloat16)
    zero_q = jnp.zeros((tq, LANES), jnp.bfloat16)

    def tile_body(t, carry):
        g = step * tiles + t
        case = jnp.where(g == 0, 0, jnp.where(g == tiles_in_seq - 1, 2, 1))
        bias = bias_ref[case]
        r0 = pl.multiple_of(t * tq, tq)
        for c in range(q_tiles):
            kc = c if kv_tiles == q_tiles else 0
            ksl = slice(kc * LANES, (kc + 1) * LANES)
            qp = q_ref[pl.ds(r0, tq), c * LANES:(c + 1) * LANES]
            q2 = jnp.concatenate([jnp.where(low, qp, zero_q), jnp.where(low, zero_q, qp)], axis=0)
            kp = kbuf[pl.ds(r0, tk), ksl]
            vext = jnp.concatenate([vbuf[pl.ds(r0, tk), ksl], ones], axis=1)
            s = lax.dot_general(q2, kp, (((1,), (1,)), ((), ())), preferred_element_type=jnp.float32)
            s_lo = s[:tq] + bias
            s_hi = s[tq:] + bias
            m_lo = jnp.max(s_lo, axis=-1, keepdims=True)
            m_hi = jnp.max(s_hi, axis=-1, keepdims=True)
            if sinks:
                sink_lo = sink_ref[A_HEAD_ORDER[2 * c]]
                sink_hi = sink_ref[A_HEAD_ORDER[2 * c + 1]]
                m_lo = jnp.maximum(m_lo, sink_lo)
                m_hi = jnp.maximum(m_hi, sink_hi)
            p = jnp.concatenate([jnp.exp(s_lo - m_lo), jnp.exp(s_hi - m_hi)], axis=0).astype(jnp.bfloat16)
            o2 = jnp.dot(p, vext, preferred_element_type=jnp.float32)
            num = jnp.where(low, o2[:tq, :LANES], o2[tq:, :LANES])
            den = jnp.where(low, o2[:tq, LANES:], o2[tq:, LANES:])
            m = jnp.where(low, m_lo, m_hi)
            if sinks:
                den = den + jnp.exp(jnp.where(low, sink_lo, sink_hi) - m)
            o_ref[pl.ds(r0, tq), c * LANES:(c + 1) * LANES] = (num / den).astype(o_ref.dtype)
            if emit_lse:
                lse_ref[pl.ds(r0, tq), c * LANES:(c + 1) * LANES] = m + jnp.log(den)
        return carry

    lax.fori_loop(0, tiles, tile_body, 0)


def _const_spec(shape):
    zeros = (0,) * len(shape)
    return pl.BlockSpec(shape, lambda *_: zeros, pipeline_mode=pl.Buffered(1))


def _stack_ffn_weights(w_gate, w_up, w_down):
    d, ff = w_gate.shape
    nc = ff // FF_CHUNK
    wg = w_gate.reshape(d, nc, FF_CHUNK).transpose(1, 0, 2)
    wu = w_up.reshape(d, nc, FF_CHUNK).transpose(1, 0, 2)
    wgu = jnp.concatenate([wg, wu], axis=-1).astype(jnp.bfloat16)
    wd = w_down.reshape(nc, FF_CHUNK, d).astype(jnp.bfloat16)
    return wgu, wd


def _ffn_in_proj(x3d, pos3d, freq, g1, wgu, wd, gm, w_in):
    batch, seq_len, d = x3d.shape
    tm = TOKEN_TILE
    tiles = seq_len // tm
    bf = jnp.bfloat16
    row = lambda w: pl.BlockSpec((None, tm, w), lambda b, i: (b, i, 0))
    dil = lambda dd: pl.BlockSpec((None, dd, tm // dd, B_W), lambda b, i: (b, 0, i, 0))
    nat = lambda w, dt: jax.ShapeDtypeStruct((batch, seq_len, w), dt)
    dil_shape = lambda dd: jax.ShapeDtypeStruct((batch, dd, seq_len // dd, B_W), bf)
    out_specs = [row(d), row(A_Q_W), row(A_KV_W), row(A_KV_W), row(B_W), row(B_W), row(B_W)]
    out_shape = [nat(d, jnp.float32), nat(A_Q_W, bf), nat(A_KV_W, bf), nat(A_KV_W, bf),
                 nat(B_W, bf), nat(B_W, bf), nat(B_W, bf)]
    for _ in range(3):
        out_specs += [dil(dd) for dd in DILATIONS]
        out_shape += [dil_shape(dd) for dd in DILATIONS]
    return pl.pallas_call(
        _ffn_in_proj_kernel,
        grid=(batch, tiles),
        in_specs=[row(d), row(1), _const_spec(freq.shape), _const_spec(g1.shape), _const_spec(wgu.shape),
                  _const_spec(wd.shape), _const_spec(gm.shape), _const_spec(w_in.shape)],
        out_specs=out_specs,
        out_shape=out_shape,
        scratch_shapes=[pltpu.VMEM((tm, d), jnp.float32), pltpu.VMEM((B_W // LANES, tm, LANES), jnp.float32)],
        compiler_params=pltpu.CompilerParams(dimension_semantics=("arbitrary", "arbitrary"),
                                             vmem_limit_bytes=VMEM_LIMIT),
        name="ffn_in_proj",
    )(x3d, pos3d, freq, g1, wgu, wd, gm, w_in)


def _out_ffn(x1, a, b_parts, woa, wob, g2, wgu, wd, gf, final_norm):
    batch, seq_len, d = x1.shape
    tm = TOKEN_TILE
    tiles = seq_len // tm
    row = lambda w: pl.BlockSpec((None, tm, w), lambda b, i: (b, i, 0))
    dil = lambda dd: pl.BlockSpec((None, dd, tm // dd, B_W), lambda b, i: (b, 0, i, 0))
    (o1, l1), (o4, l4), (o16, l16) = b_parts
    d4, d16 = DILATIONS
    slab = lambda: pltpu.VMEM((B_W // LANES, tm, LANES), jnp.float32)
    return pl.pallas_call(
        functools.partial(_out_ffn_kernel, final_norm=final_norm),
        grid=(batch, tiles),
        in_specs=[row(d), row(A_Q_W), row(B_W), row(B_W), dil(d4), dil(d4), dil(d16), dil(d16),
                  _const_spec(woa.shape), _const_spec(wob.shape), _const_spec(g2.shape),
                  _const_spec(wgu.shape), _const_spec(wd.shape), _const_spec(gf.shape)],
        out_specs=row(d),
        out_shape=jax.ShapeDtypeStruct((batch, seq_len, d), jnp.float32),
        scratch_shapes=[pltpu.VMEM((tm, d), jnp.float32), slab(), slab(), slab(), slab()],
        compiler_params=pltpu.CompilerParams(dimension_semantics=("arbitrary", "arbitrary"),
                                             vmem_limit_bytes=VMEM_LIMIT),
        name="out_ffn",
    )(x1, a, o1, l1, o4, l4, o16, l16, woa, wob, g2, wgu, wd, gf)


def _banded_attention(q, k, v, sink, *, tq, halo, emit_lse, name):
    batch, groups, seq_len, wq = q.shape
    wkv = k.shape[-1]
    rows = ATTN_STEP_ROWS
    steps = seq_len // rows
    ratio = rows // halo
    halo_blocks = seq_len // halo
    tile = lambda w: pl.BlockSpec((None, None, rows, w), lambda b, r, i: (b, r, i, 0))
    prev = pl.BlockSpec((None, None, halo, wkv), lambda b, r, i: (b, r, jnp.maximum(i * ratio - 1, 0), 0))
    nxt = pl.BlockSpec((None, None, halo, wkv),
                       lambda b, r, i: (b, r, jnp.minimum((i + 1) * ratio, halo_blocks - 1), 0))
    kv_specs = [prev, tile(wkv), nxt]
    in_specs = [tile(wq), *kv_specs, *kv_specs]
    args = [q, k, k, k, v, v, v]
    if sink is not None:
        in_specs = [pl.BlockSpec(memory_space=pltpu.SMEM)] + in_specs
        args = [sink] + args
    out_specs = [tile(wq)]
    out_shape = [jax.ShapeDtypeStruct(q.shape, jnp.bfloat16)]
    if emit_lse:
        out_specs.append(tile(wq))
        out_shape.append(jax.ShapeDtypeStruct(q.shape, jnp.float32))
    tk = tq + 2 * halo
    res = pl.pallas_call(
        functools.partial(_attn_kernel, seq_len=seq_len, tq=tq, halo=halo, sinks=sink is not None,
                          emit_lse=emit_lse),
        grid=(batch, groups, steps),
        in_specs=in_specs,
        out_specs=out_specs,
        out_shape=out_shape,
        scratch_shapes=[pltpu.VMEM((rows + 2 * halo, wkv), jnp.bfloat16),
                        pltpu.VMEM((rows + 2 * halo, wkv), jnp.bfloat16),
                        pltpu.VMEM((3, tq, tk), jnp.float32)],
        compiler_params=pltpu.CompilerParams(dimension_semantics=("arbitrary",) * 3),
        name=name,
    )(*args)
    return res


def kernel(x, positions, norm_ffn1, w_gate1, w_up1, w_down1, norm_mix, w_in, a_sink, w_out,
           norm_ffn2, w_gate2, w_up2, w_down2, norm_final):
    batch, seq_len, d = x.shape
    depth = norm_ffn1.shape[0]
    bf = jnp.bfloat16

    inv_freq = 1.0 / (ROPE_THETA ** (jnp.arange(0, HEAD_DIM, 2, dtype=jnp.float32) / HEAD_DIM))
    freq = jnp.tile(inv_freq, LANES // (HEAD_DIM // 2))[None, :]
    pos3d = positions[:, :, None]
    order = jnp.array(A_HEAD_ORDER)
    xc = x

    for l in range(depth):
        wgu1, wd1 = _stack_ffn_weights(w_gate1[l], w_up1[l], w_down1[l])
        wgu2, wd2 = _stack_ffn_weights(w_gate2[l], w_up2[l], w_down2[l])
        w_in_l = w_in[l]
        w_aq = w_in_l[:, :A_Q_W].reshape(d, A_Q_HEADS, HEAD_DIM)[:, order].reshape(d, A_Q_W)
        w_in_l = jnp.concatenate([w_aq, w_in_l[:, A_Q_W:]], axis=1).astype(bf)
        w_out_l = w_out[l]
        woa = w_out_l[:A_Q_W].reshape(A_Q_HEADS, HEAD_DIM, d)[order].reshape(A_Q_W, d).astype(bf)
        wob = w_out_l[A_Q_W:].astype(bf)

        outs = _ffn_in_proj(xc, pos3d, freq, norm_ffn1[l][None, :], wgu1, wd1, norm_mix[l][None, :], w_in_l)
        x1, aq, ak, av, bq, bk, bv = outs[:7]
        n_dil = len(DILATIONS)
        bq_d, bk_d, bv_d = outs[7:7 + n_dil], outs[7 + n_dil:7 + 2 * n_dil], outs[7 + 2 * n_dil:]

        (a_out,) = _banded_attention(aq[:, None], ak[:, None], av[:, None], a_sink[l],
                                     tq=2 * A_HALF_WINDOW, halo=A_HALF_WINDOW, emit_lse=False, name="attn_a")
        b_parts = []
        for window, dilation in B_PATTERNS:
            if dilation == 1:
                qkv = (bq[:, None], bk[:, None], bv[:, None])
            else:
                j = DILATIONS.index(dilation)
                qkv = (bq_d[j], bk_d[j], bv_d[j])
            halo = window // (2 * dilation)
            o, lse = _banded_attention(*qkv, None, tq=2 * halo, halo=halo, emit_lse=True,
                                       name=f"attn_b_d{dilation}")
            if dilation == 1:
                o, lse = o[:, 0], lse[:, 0]
            b_parts.append((o, lse))

        xc = _out_ffn(x1, a_out[:, 0], b_parts, woa, wob, norm_ffn2[l][None, :], wgu2, wd2,
                      norm_final[None, :], final_norm=l == depth - 1)
    return xc
```

```python
import functools

import jax
import jax.numpy as jnp
from jax import lax
from jax.experimental import pallas as pl
from jax.experimental.pallas import tpu as pltpu

HEAD_DIM = 64
A_Q_HEADS = 8
A_KV_HEADS = 2
A_HALF_WINDOW = 128
B_HEADS = 8
B_PATTERNS = ((128, 1), (512, 4), (2048, 16))
ROPE_THETA = 10000.0
NORM_EPS = 1e-6
FFN_RES_WEIGHT = 0.5

A_Q_W = A_Q_HEADS * HEAD_DIM
A_KV_W = A_KV_HEADS * HEAD_DIM
B_W = B_HEADS * HEAD_DIM

LANES = 128
NEG_BIG = -1e30
LOG2E = 1.4426950408889634
TOKEN_TILE = 512
FF_CHUNK = 256
ATTN_STEP_ROWS = 512
VMEM_LIMIT = 56 * 1024 * 1024
A_HEAD_ORDER = (0, 4, 1, 5, 2, 6, 3, 7)
DILATIONS = tuple(d for _, d in B_PATTERNS if d > 1)


def _rms_norm(x, g):
    y = x * lax.rsqrt(jnp.mean(x * x, axis=-1, keepdims=True) + NORM_EPS)
    return y * g


def _swiglu_accumulate(h_bf16, wg_ref, wu_ref, wd_ref, acc_ref):
    d_ff = wg_ref.shape[1]
    for c in range(d_ff // FF_CHUNK):
        sl = slice(c * FF_CHUNK, (c + 1) * FF_CHUNK)
        g = jnp.dot(h_bf16, wg_ref[:, sl], preferred_element_type=jnp.float32)
        u = jnp.dot(h_bf16, wu_ref[:, sl], preferred_element_type=jnp.float32)
        a = (g * jax.nn.sigmoid(g) * u).astype(jnp.bfloat16)
        down = jnp.dot(a, wd_ref[sl, :], preferred_element_type=jnp.float32)
        if c == 0:
            acc_ref[...] = down
        else:
            acc_ref[...] += down


def _rope(t, cos, sin_signed, first_half):
    parts = []
    for j in range(t.shape[1] // LANES):
        tj = t[:, j * LANES:(j + 1) * LANES]
        partner = jnp.where(first_half, pltpu.roll(tj, LANES - HEAD_DIM // 2, 1), pltpu.roll(tj, HEAD_DIM // 2, 1))
        parts.append(tj * cos + partner * sin_signed)
    return parts[0] if len(parts) == 1 else jnp.concatenate(parts, axis=1)


def _write_deinterleaved(val, slab_ref, stage_ref, out_refs):
    tm, width = val.shape
    n_slabs = width // LANES
    d4, d16 = DILATIONS
    ratio = d16 // d4
    rows4 = tm // d4
    for c in range(n_slabs):
        slab_ref[c] = val[:, c * LANES:(c + 1) * LANES]
    for r in range(d4):
        for c in range(n_slabs):
            part = slab_ref[c, pl.ds(r, rows4, stride=d4), :]
            out_refs[d4][r, :, c * LANES:(c + 1) * LANES] = part.astype(out_refs[d4].dtype)
            stage_ref[r * n_slabs + c] = part
    for r in range(d4):
        for q in range(ratio):
            for c in range(n_slabs):
                part = stage_ref[r * n_slabs + c, pl.ds(q, rows4 // ratio, stride=ratio), :]
                out_refs[d16][q * d4 + r, :, c * LANES:(c + 1) * LANES] = part.astype(out_refs[d16].dtype)


def _ffn_in_proj_kernel(x_ref, pos_ref, freq_ref, g1_ref, wg_ref, wu_ref, wd_ref, gm_ref, win_ref,
                        x1_ref, aq_ref, ak_ref, av_ref, bq_ref, bk_ref, bv_ref, *rest):
    n_dil = len(DILATIONS)
    bq_d = dict(zip(DILATIONS, rest[0:n_dil]))
    bk_d = dict(zip(DILATIONS, rest[n_dil:2 * n_dil]))
    bv_d = dict(zip(DILATIONS, rest[2 * n_dil:3 * n_dil]))
    acc_ref, slab_ref, stage_ref = rest[3 * n_dil:]

    x = x_ref[...]
    h = _rms_norm(x, g1_ref[...]).astype(jnp.bfloat16)
    _swiglu_accumulate(h, wg_ref, wu_ref, wd_ref, acc_ref)
    x1 = x + FFN_RES_WEIGHT * acc_ref[...]
    x1_ref[...] = x1
    h2 = _rms_norm(x1, gm_ref[...]).astype(jnp.bfloat16)

    ang = pos_ref[...].astype(jnp.float32) * freq_ref[...]
    lane = lax.broadcasted_iota(jnp.int32, ang.shape, 1)
    first_half = (lane % HEAD_DIM) < (HEAD_DIM // 2)
    cos = jnp.cos(ang)
    sin = jnp.sin(ang)
    sin_signed = jnp.where(first_half, -sin, sin)
    scale = HEAD_DIM ** -0.5 * LOG2E

    def proj(lo, width):
        return jnp.dot(h2, win_ref[:, lo:lo + width], preferred_element_type=jnp.float32)

    offs = {}
    o = 0
    for name, width in (("aq", A_Q_W), ("ak", A_KV_W), ("av", A_KV_W), ("bq", B_W), ("bk", B_W), ("bv", B_W)):
        offs[name] = (o, width)
        o += width
    rope = lambda t: _rope(t, cos, sin_signed, first_half)

    bq = rope(proj(*offs["bq"])) * scale
    bq_ref[...] = bq.astype(bq_ref.dtype)
    _write_deinterleaved(bq, slab_ref, stage_ref, bq_d)
    bk = rope(proj(*offs["bk"]))
    bk_ref[...] = bk.astype(bk_ref.dtype)
    _write_deinterleaved(bk, slab_ref, stage_ref, bk_d)
    bv = proj(*offs["bv"])
    bv_ref[...] = bv.astype(bv_ref.dtype)
    _write_deinterleaved(bv, slab_ref, stage_ref, bv_d)
    aq_ref[...] = (rope(proj(*offs["aq"])) * scale).astype(aq_ref.dtype)
    ak_ref[...] = rope(proj(*offs["ak"])).astype(ak_ref.dtype)
    av_ref[...] = proj(*offs["av"]).astype(av_ref.dtype)


def _interleave_into(slab_ref, blk_ref, d):
    rows = blk_ref.shape[1]
    for r in range(d):
        for c in range(slab_ref.shape[0]):
            slab_ref[c, pl.ds(r, rows, stride=d), :] = blk_ref[r, :, c * LANES:(c + 1) * LANES].astype(jnp.float32)


def _out_ffn_kernel(x1_ref, a_ref, o1_ref, l1_ref, o4_ref, l4_ref, o16_ref, l16_ref,
                    woa_ref, wob_ref, g2_ref, wg_ref, wu_ref, wd_ref, gf_ref,
                    out_ref, acc_ref, so4_ref, sl4_ref, so16_ref, sl16_ref, *, final_norm):
    d4, d16 = DILATIONS
    _interleave_into(so4_ref, o4_ref, d4)
    _interleave_into(sl4_ref, l4_ref, d4)
    _interleave_into(so16_ref, o16_ref, d16)
    _interleave_into(sl16_ref, l16_ref, d16)

    mix = jnp.dot(a_ref[...], woa_ref[...], preferred_element_type=jnp.float32)
    for c in range(B_W // LANES):
        sl = slice(c * LANES, (c + 1) * LANES)
        la, lb, lc = l1_ref[:, sl], sl4_ref[c], sl16_ref[c]
        m = jnp.maximum(jnp.maximum(la, lb), lc)
        wa, wb, wc = jnp.exp2(la - m), jnp.exp2(lb - m), jnp.exp2(lc - m)
        num = wa * o1_ref[:, sl].astype(jnp.float32) + wb * so4_ref[c] + wc * so16_ref[c]
        b = (num / (wa + wb + wc)).astype(jnp.bfloat16)
        mix += jnp.dot(b, wob_ref[sl, :], preferred_element_type=jnp.float32)
    x2 = x1_ref[...] + mix
    h = _rms_norm(x2, g2_ref[...]).astype(jnp.bfloat16)
    _swiglu_accumulate(h, wg_ref, wu_ref, wd_ref, acc_ref)
    x3 = x2 + FFN_RES_WEIGHT * acc_ref[...]
    out_ref[...] = _rms_norm(x3, gf_ref[...]) if final_norm else x3


def _attn_kernel(*refs, seq_len, tq, halo, sinks, emit_lse):
    if sinks:
        sink_ref, refs = refs[0], refs[1:]
    q_ref, kp_ref, kc_ref, kn_ref, vp_ref, vc_ref, vn_ref, o_ref = refs[:8]
    refs = refs[8:]
    if emit_lse:
        lse_ref, refs = refs[0], refs[1:]
    kbuf, vbuf, bias_ref = refs

    rows = q_ref.shape[0]
    tk = tq + 2 * halo
    tiles = rows // tq
    tiles_in_seq = seq_len // tq
    kv_tiles = kbuf.shape[1] // LANES
    q_tiles = q_ref.shape[1] // LANES
    step = pl.program_id(2)

    @pl.when((pl.program_id(0) == 0) & (pl.program_id(1) == 0) & (step == 0))
    def _():
        row = lax.broadcasted_iota(jnp.int32, (tq, tk), 0)
        col = lax.broadcasted_iota(jnp.int32, (tq, tk), 1)
        band = jnp.abs(row + halo - col) <= halo
        bias_ref[0] = jnp.where(band & (col >= halo), 0.0, NEG_BIG)
        bias_ref[1] = jnp.where(band, 0.0, NEG_BIG)
        bias_ref[2] = jnp.where(band & (col < tq + halo), 0.0, NEG_BIG)

    kbuf[0:halo, :] = kp_ref[...]
    kbuf[halo:halo + rows, :] = kc_ref[...]
    kbuf[halo + rows:, :] = kn_ref[...]
    vbuf[0:halo, :] = vp_ref[...]
    vbuf[halo:halo + rows, :] = vc_ref[...]
    vbuf[halo + rows:, :] = vn_ref[...]

    lane = lax.broadcasted_iota(jnp.int32, (tq, LANES), 1)
    low = lane < HEAD_DIM
    ones = jnp.ones((tk, LANES), jnp.bfloat16)
    zero_q = jnp.zeros((tq, LANES), jnp.bfloat16)

    def tile_body(t, carry):
        g = step * tiles + t
        case = jnp.where(g == 0, 0, jnp.where(g == tiles_in_seq - 1, 2, 1))
        bias = bias_ref[case]
        r0 = pl.multiple_of(t * tq, tq)
        for c in range(q_tiles):
            kc = c if kv_tiles == q_tiles else 0
            ksl = slice(kc * LANES, (kc + 1) * LANES)
            qp = q_ref[pl.ds(r0, tq), c * LANES:(c + 1) * LANES]
            q2 = jnp.concatenate([jnp.where(low, qp, zero_q), jnp.where(low, zero_q, qp)], axis=0)
            kp = kbuf[pl.ds(r0, tk), ksl]
            vext = jnp.concatenate([vbuf[pl.ds(r0, tk), ksl], ones], axis=1)
            s = lax.dot_general(q2, kp, (((1,), (1,)), ((), ())), preferred_element_type=jnp.float32)
            s_lo = s[:tq] + bias
            s_hi = s[tq:] + bias
            m_lo = jnp.max(s_lo, axis=-1, keepdims=True)
            m_hi = jnp.max(s_hi, axis=-1, keepdims=True)
            if sinks:
                sink_lo = sink_ref[A_HEAD_ORDER[2 * c]] * LOG2E
                sink_hi = sink_ref[A_HEAD_ORDER[2 * c + 1]] * LOG2E
                m_lo = jnp.maximum(m_lo, sink_lo)
                m_hi = jnp.maximum(m_hi, sink_hi)
            p = jnp.concatenate([jnp.exp2(s_lo - m_lo), jnp.exp2(s_hi - m_hi)], axis=0).astype(jnp.bfloat16)
            o2 = jnp.dot(p, vext, preferred_element_type=jnp.float32)
            num = jnp.where(low, o2[:tq, :LANES], o2[tq:, :LANES])
            den = jnp.where(low, o2[:tq, LANES:], o2[tq:, LANES:])
            m = jnp.where(low, m_lo, m_hi)
            if sinks:
                den = den + jnp.exp2(jnp.where(low, sink_lo, sink_hi) - m)
            o_ref[pl.ds(r0, tq), c * LANES:(c + 1) * LANES] = (num / den).astype(o_ref.dtype)
            if emit_lse:
                lse_ref[pl.ds(r0, tq), c * LANES:(c + 1) * LANES] = m + jnp.log(den) * LOG2E
        return carry

    lax.fori_loop(0, tiles, tile_body, 0, unroll=True)


def _const_spec(shape):
    zeros = (0,) * len(shape)
    return pl.BlockSpec(shape, lambda *_: zeros, pipeline_mode=pl.Buffered(1))


def _ffn_in_proj(x3d, pos3d, freq, g1, ffn_w, gm, w_in):
    batch, seq_len, d = x3d.shape
    tm = TOKEN_TILE
    tiles = seq_len // tm
    bf = jnp.bfloat16
    row = lambda w: pl.BlockSpec((None, tm, w), lambda b, i: (b, i, 0))
    dil = lambda dd: pl.BlockSpec((None, dd, tm // dd, B_W), lambda b, i: (b, 0, i, 0))
    nat = lambda w, dt: jax.ShapeDtypeStruct((batch, seq_len, w), dt)
    dil_shape = lambda dd: jax.ShapeDtypeStruct((batch, dd, seq_len // dd, B_W), bf)
    out_specs = [row(d), row(A_Q_W), row(A_KV_W), row(A_KV_W), row(B_W), row(B_W), row(B_W)]
    out_shape = [nat(d, jnp.float32), nat(A_Q_W, bf), nat(A_KV_W, bf), nat(A_KV_W, bf),
                 nat(B_W, bf), nat(B_W, bf), nat(B_W, bf)]
    for _ in range(3):
        out_specs += [dil(dd) for dd in DILATIONS]
        out_shape += [dil_shape(dd) for dd in DILATIONS]
    return pl.pallas_call(
        _ffn_in_proj_kernel,
        grid=(batch, tiles),
        in_specs=[row(d), row(1), _const_spec(freq.shape), _const_spec(g1.shape),
                  *[_const_spec(w.shape) for w in ffn_w], _const_spec(gm.shape), _const_spec(w_in.shape)],
        out_specs=out_specs,
        out_shape=out_shape,
        scratch_shapes=[pltpu.VMEM((tm, d), jnp.float32), pltpu.VMEM((B_W // LANES, tm, LANES), jnp.float32),
                        pltpu.VMEM((DILATIONS[0] * B_W // LANES, tm // DILATIONS[0], LANES), jnp.float32)],
        compiler_params=pltpu.CompilerParams(dimension_semantics=("arbitrary", "arbitrary"),
                                             vmem_limit_bytes=VMEM_LIMIT),
        name="ffn_in_proj",
    )(x3d, pos3d, freq, g1, *ffn_w, gm, w_in)


def _out_ffn(x1, a, b_parts, woa, wob, g2, ffn_w, gf, final_norm):
    batch, seq_len, d = x1.shape
    tm = TOKEN_TILE
    tiles = seq_len // tm
    row = lambda w: pl.BlockSpec((None, tm, w), lambda b, i: (b, i, 0))
    dil = lambda dd: pl.BlockSpec((None, dd, tm // dd, B_W), lambda b, i: (b, 0, i, 0))
    (o1, l1), (o4, l4), (o16, l16) = b_parts
    d4, d16 = DILATIONS
    slab = lambda: pltpu.VMEM((B_W // LANES, tm, LANES), jnp.float32)
    return pl.pallas_call(
        functools.partial(_out_ffn_kernel, final_norm=final_norm),
        grid=(batch, tiles),
        in_specs=[row(d), row(A_Q_W), row(B_W), row(B_W), dil(d4), dil(d4), dil(d16), dil(d16),
                  _const_spec(woa.shape), _const_spec(wob.shape), _const_spec(g2.shape),
                  *[_const_spec(w.shape) for w in ffn_w], _const_spec(gf.shape)],
        out_specs=row(d),
        out_shape=jax.ShapeDtypeStruct((batch, seq_len, d), jnp.float32),
        scratch_shapes=[pltpu.VMEM((tm, d), jnp.float32), slab(), slab(), slab(), slab()],
        compiler_params=pltpu.CompilerParams(dimension_semantics=("arbitrary", "arbitrary"),
                                             vmem_limit_bytes=VMEM_LIMIT),
        name="out_ffn",
    )(x1, a, o1, l1, o4, l4, o16, l16, woa, wob, g2, *ffn_w, gf)


def _banded_attention(q, k, v, sink, *, tq, halo, emit_lse, name):
    batch, groups, seq_len, wq = q.shape
    wkv = k.shape[-1]
    rows = ATTN_STEP_ROWS
    steps = seq_len // rows
    ratio = rows // halo
    halo_blocks = seq_len // halo
    tile = lambda w: pl.BlockSpec((None, None, rows, w), lambda b, r, i: (b, r, i, 0))
    prev = pl.BlockSpec((None, None, halo, wkv), lambda b, r, i: (b, r, jnp.maximum(i * ratio - 1, 0), 0))
    nxt = pl.BlockSpec((None, None, halo, wkv),
                       lambda b, r, i: (b, r, jnp.minimum((i + 1) * ratio, halo_blocks - 1), 0))
    kv_specs = [prev, tile(wkv), nxt]
    in_specs = [tile(wq), *kv_specs, *kv_specs]
    args = [q, k, k, k, v, v, v]
    if sink is not None:
        in_specs = [pl.BlockSpec(memory_space=pltpu.SMEM)] + in_specs
        args = [sink] + args
    out_specs = [tile(wq)]
    out_shape = [jax.ShapeDtypeStruct(q.shape, jnp.bfloat16)]
    if emit_lse:
        out_specs.append(tile(wq))
        out_shape.append(jax.ShapeDtypeStruct(q.shape, jnp.float32))
    tk = tq + 2 * halo
    res = pl.pallas_call(
        functools.partial(_attn_kernel, seq_len=seq_len, tq=tq, halo=halo, sinks=sink is not None,
                          emit_lse=emit_lse),
        grid=(batch, groups, steps),
        in_specs=in_specs,
        out_specs=out_specs,
        out_shape=out_shape,
        scratch_shapes=[pltpu.VMEM((rows + 2 * halo, wkv), jnp.bfloat16),
                        pltpu.VMEM((rows + 2 * halo, wkv), jnp.bfloat16),
                        pltpu.VMEM((3, tq, tk), jnp.float32)],
        compiler_params=pltpu.CompilerParams(dimension_semantics=("arbitrary",) * 3),
        name=name,
    )(*args)
    return res


def kernel(x, positions, norm_ffn1, w_gate1, w_up1, w_down1, norm_mix, w_in, a_sink, w_out,
           norm_ffn2, w_gate2, w_up2, w_down2, norm_final):
    batch, seq_len, d = x.shape
    depth = norm_ffn1.shape[0]
    bf = jnp.bfloat16

    inv_freq = 1.0 / (ROPE_THETA ** (jnp.arange(0, HEAD_DIM, 2, dtype=jnp.float32) / HEAD_DIM))
    freq = jnp.tile(inv_freq, LANES // (HEAD_DIM // 2))[None, :]
    pos3d = positions[:, :, None]
    order = jnp.array(A_HEAD_ORDER)
    xc = x

    for l in range(depth):
        ffn1 = tuple(w[l].astype(bf) for w in (w_gate1, w_up1, w_down1))
        ffn2 = tuple(w[l].astype(bf) for w in (w_gate2, w_up2, w_down2))
        w_in_l = w_in[l]
        w_aq = w_in_l[:, :A_Q_W].reshape(d, A_Q_HEADS, HEAD_DIM)[:, order].reshape(d, A_Q_W)
        w_in_l = jnp.concatenate([w_aq, w_in_l[:, A_Q_W:]], axis=1).astype(bf)
        w_out_l = w_out[l]
        woa = w_out_l[:A_Q_W].reshape(A_Q_HEADS, HEAD_DIM, d)[order].reshape(A_Q_W, d).astype(bf)
        wob = w_out_l[A_Q_W:].astype(bf)

        outs = _ffn_in_proj(xc, pos3d, freq, norm_ffn1[l][None, :], ffn1, norm_mix[l][None, :], w_in_l)
        x1, aq, ak, av, bq, bk, bv = outs[:7]
        n_dil = len(DILATIONS)
        bq_d, bk_d, bv_d = outs[7:7 + n_dil], outs[7 + n_dil:7 + 2 * n_dil], outs[7 + 2 * n_dil:]

        (a_out,) = _banded_attention(aq[:, None], ak[:, None], av[:, None], a_sink[l],
                                     tq=2 * A_HALF_WINDOW, halo=A_HALF_WINDOW, emit_lse=False, name="attn_a")
        b_parts = []
        for window, dilation in B_PATTERNS:
            if dilation == 1:
                qkv = (bq[:, None], bk[:, None], bv[:, None])
            else:
                j = DILATIONS.index(dilation)
                qkv = (bq_d[j], bk_d[j], bv_d[j])
            halo = window // (2 * dilation)
            o, lse = _banded_attention(*qkv, None, tq=2 * halo, halo=halo, emit_lse=True,
                                       name=f"attn_b_d{dilation}")
            if dilation == 1:
                o, lse = o[:, 0], lse[:, 0]
            b_parts.append((o, lse))

        xc = _out_ffn(x1, a_out[:, 0], b_parts, woa, wob, norm_ffn2[l][None, :], ffn2,
                      norm_final[None, :], final_norm=l == depth - 1)
    return xc
```

```python
import functools

import jax
import jax.numpy as jnp
from jax import lax
from jax.experimental import pallas as pl
from jax.experimental.pallas import tpu as pltpu

HEAD_DIM = 64
A_Q_HEADS = 8
A_KV_HEADS = 2
A_HALF_WINDOW = 128
B_HEADS = 8
B_PATTERNS = ((128, 1), (512, 4), (2048, 16))
ROPE_THETA = 10000.0
NORM_EPS = 1e-6
FFN_RES_WEIGHT = 0.5

A_Q_W = A_Q_HEADS * HEAD_DIM
A_KV_W = A_KV_HEADS * HEAD_DIM
B_W = B_HEADS * HEAD_DIM

LANES = 128
NEG_BIG = -1e30
LOG2E = 1.4426950408889634
TOKEN_TILE = 512
FF_CHUNK = 256
ATTN_A_STEP_ROWS = 1024
ATTN_B_STEP_ROWS = 2048
VMEM_LIMIT = 56 * 1024 * 1024
A_HEAD_ORDER = (0, 4, 1, 5, 2, 6, 3, 7)
DILATIONS = tuple(d for _, d in B_PATTERNS if d > 1)


def _rms_norm(x, g):
    y = x * lax.rsqrt(jnp.mean(x * x, axis=-1, keepdims=True) + NORM_EPS)
    return y * g


def _swiglu_accumulate(h_ref, wg_ref, wu_ref, wd_ref, acc_ref):
    d_ff = wg_ref.shape[1]
    for c in range(d_ff // FF_CHUNK):
        sl = slice(c * FF_CHUNK, (c + 1) * FF_CHUNK)
        g = jnp.dot(h_ref[...], wg_ref[:, sl], preferred_element_type=jnp.float32)
        u = jnp.dot(h_ref[...], wu_ref[:, sl], preferred_element_type=jnp.float32)
        a = (g * jax.nn.sigmoid(g) * u).astype(jnp.bfloat16)
        down = jnp.dot(a, wd_ref[sl, :], preferred_element_type=jnp.float32)
        if c == 0:
            acc_ref[...] = down
        else:
            acc_ref[...] += down


def _rope(t, cos, sin_signed, first_half):
    parts = []
    for j in range(t.shape[1] // LANES):
        tj = t[:, j * LANES:(j + 1) * LANES]
        partner = jnp.where(first_half, pltpu.roll(tj, LANES - HEAD_DIM // 2, 1), pltpu.roll(tj, HEAD_DIM // 2, 1))
        parts.append(tj * cos + partner * sin_signed)
    return parts[0] if len(parts) == 1 else jnp.concatenate(parts, axis=1)


def _write_deinterleaved(val, slab_ref, stage_ref, out_refs):
    tm, width = val.shape
    n_slabs = width // LANES
    d4, d16 = DILATIONS
    ratio = d16 // d4
    rows4 = tm // d4
    for c in range(n_slabs):
        slab_ref[c] = val[:, c * LANES:(c + 1) * LANES]
    for r in range(d4):
        for c in range(n_slabs):
            part = slab_ref[c, pl.ds(r, rows4, stride=d4), :]
            out_refs[d4][r, :, c * LANES:(c + 1) * LANES] = part.astype(out_refs[d4].dtype)
            stage_ref[r * n_slabs + c] = part
    for r in range(d4):
        for q in range(ratio):
            for c in range(n_slabs):
                part = stage_ref[r * n_slabs + c, pl.ds(q, rows4 // ratio, stride=ratio), :]
                out_refs[d16][q * d4 + r, :, c * LANES:(c + 1) * LANES] = part.astype(out_refs[d16].dtype)


def _ffn_in_proj_kernel(x_ref, pos_ref, freq_ref, g1_ref, wg_ref, wu_ref, wd_ref, gm_ref, win_ref,
                        x1_ref, aq_ref, ak_ref, av_ref, bq_ref, bk_ref, bv_ref, *rest):
    n_dil = len(DILATIONS)
    bq_d = dict(zip(DILATIONS, rest[0:n_dil]))
    bk_d = dict(zip(DILATIONS, rest[n_dil:2 * n_dil]))
    bv_d = dict(zip(DILATIONS, rest[2 * n_dil:3 * n_dil]))
    h_ref, acc_ref, slab_ref, stage_ref = rest[3 * n_dil:]

    x = x_ref[...]
    h_ref[...] = _rms_norm(x, g1_ref[...]).astype(jnp.bfloat16)
    _swiglu_accumulate(h_ref, wg_ref, wu_ref, wd_ref, acc_ref)
    x1 = x + FFN_RES_WEIGHT * acc_ref[...]
    x1_ref[...] = x1
    h_ref[...] = _rms_norm(x1, gm_ref[...]).astype(jnp.bfloat16)

    ang = pos_ref[...].astype(jnp.float32) * freq_ref[...]
    lane = lax.broadcasted_iota(jnp.int32, ang.shape, 1)
    first_half = (lane % HEAD_DIM) < (HEAD_DIM // 2)
    cos = jnp.cos(ang)
    sin = jnp.sin(ang)
    sin_signed = jnp.where(first_half, -sin, sin)
    scale = HEAD_DIM ** -0.5 * LOG2E

    def proj(lo, width):
        return jnp.dot(h_ref[...], win_ref[:, lo:lo + width], preferred_element_type=jnp.float32)

    offs = {}
    o = 0
    for name, width in (("aq", A_Q_W), ("ak", A_KV_W), ("av", A_KV_W), ("bq", B_W), ("bk", B_W), ("bv", B_W)):
        offs[name] = (o, width)
        o += width
    rope = lambda t: _rope(t, cos, sin_signed, first_half)

    bq = rope(proj(*offs["bq"])) * scale
    bq_ref[...] = bq.astype(bq_ref.dtype)
    _write_deinterleaved(bq, slab_ref, stage_ref, bq_d)
    bk = rope(proj(*offs["bk"]))
    bk_ref[...] = bk.astype(bk_ref.dtype)
    _write_deinterleaved(bk, slab_ref, stage_ref, bk_d)
    bv = proj(*offs["bv"])
    bv_ref[...] = bv.astype(bv_ref.dtype)
    _write_deinterleaved(bv, slab_ref, stage_ref, bv_d)
    aq_ref[...] = (rope(proj(*offs["aq"])) * scale).astype(aq_ref.dtype)
    ak_ref[...] = rope(proj(*offs["ak"])).astype(ak_ref.dtype)
    av_ref[...] = proj(*offs["av"]).astype(av_ref.dtype)


def _interleave_into(slab_ref, blk_ref, d):
    rows = blk_ref.shape[1]
    for r in range(d):
        for c in range(slab_ref.shape[0]):
            slab_ref[c, pl.ds(r, rows, stride=d), :] = blk_ref[r, :, c * LANES:(c + 1) * LANES].astype(jnp.float32)


def _out_ffn_kernel(x1_ref, a_ref, o1_ref, l1_ref, o4_ref, l4_ref, o16_ref, l16_ref,
                    woa_ref, wob_ref, g2_ref, wg_ref, wu_ref, wd_ref, gf_ref,
                    out_ref, h_ref, acc_ref, so4_ref, sl4_ref, so16_ref, sl16_ref, *, final_norm):
    d4, d16 = DILATIONS
    _interleave_into(so4_ref, o4_ref, d4)
    _interleave_into(sl4_ref, l4_ref, d4)
    _interleave_into(so16_ref, o16_ref, d16)
    _interleave_into(sl16_ref, l16_ref, d16)

    mix = jnp.dot(a_ref[...], woa_ref[...], preferred_element_type=jnp.float32)
    for c in range(B_W // LANES):
        sl = slice(c * LANES, (c + 1) * LANES)
        la, lb, lc = l1_ref[:, sl], sl4_ref[c], sl16_ref[c]
        m = jnp.maximum(jnp.maximum(la, lb), lc)
        wa, wb, wc = jnp.exp2(la - m), jnp.exp2(lb - m), jnp.exp2(lc - m)
        num = wa * o1_ref[:, sl].astype(jnp.float32) + wb * so4_ref[c] + wc * so16_ref[c]
        b = (num / (wa + wb + wc)).astype(jnp.bfloat16)
        mix += jnp.dot(b, wob_ref[sl, :], preferred_element_type=jnp.float32)
    x2 = x1_ref[...] + mix
    h_ref[...] = _rms_norm(x2, g2_ref[...]).astype(jnp.bfloat16)
    _swiglu_accumulate(h_ref, wg_ref, wu_ref, wd_ref, acc_ref)
    x3 = x2 + FFN_RES_WEIGHT * acc_ref[...]
    out_ref[...] = _rms_norm(x3, gf_ref[...]) if final_norm else x3


def _attn_kernel(*refs, seq_len, tq, halo, sinks, emit_lse):
    if sinks:
        sink_ref, refs = refs[0], refs[1:]
    q_ref, kp_ref, kc_ref, kn_ref, vp_ref, vc_ref, vn_ref, o_ref = refs[:8]
    refs = refs[8:]
    if emit_lse:
        lse_ref, refs = refs[0], refs[1:]
    kbuf, vbuf, bias_ref = refs

    n_seqs, rows = q_ref.shape[:2]
    tk = tq + 2 * halo
    tiles = rows // tq
    tiles_in_seq = seq_len // tq
    kv_tiles = kbuf.shape[2] // LANES
    q_tiles = q_ref.shape[2] // LANES
    step = pl.program_id(2)

    @pl.when((pl.program_id(0) == 0) & (pl.program_id(1) == 0) & (step == 0))
    def _():
        row = lax.broadcasted_iota(jnp.int32, (tq, tk), 0)
        col = lax.broadcasted_iota(jnp.int32, (tq, tk), 1)
        band = jnp.abs(row + halo - col) <= halo
        bias_ref[0] = jnp.where(band & (col >= halo), 0.0, NEG_BIG)
        bias_ref[1] = jnp.where(band, 0.0, NEG_BIG)
        bias_ref[2] = jnp.where(band & (col < tq + halo), 0.0, NEG_BIG)

    kbuf[:, 0:halo, :] = kp_ref[...]
    kbuf[:, halo:halo + rows, :] = kc_ref[...]
    kbuf[:, halo + rows:, :] = kn_ref[...]
    vbuf[:, 0:halo, :] = vp_ref[...]
    vbuf[:, halo:halo + rows, :] = vc_ref[...]
    vbuf[:, halo + rows:, :] = vn_ref[...]

    lane = lax.broadcasted_iota(jnp.int32, (tq, LANES), 1)
    low = lane < HEAD_DIM
    ones = jnp.ones((tk, LANES), jnp.bfloat16)
    zero_q = jnp.zeros((tq, LANES), jnp.bfloat16)

    for n, t in [(n, t) for n in range(n_seqs) for t in range(tiles)]:
        g = step * tiles + t
        case = jnp.where(g == 0, 0, jnp.where(g == tiles_in_seq - 1, 2, 1))
        bias = bias_ref[case]
        r0 = t * tq
        for c in range(q_tiles):
            kc = c if kv_tiles == q_tiles else 0
            ksl = slice(kc * LANES, (kc + 1) * LANES)
            qp = q_ref[n, r0:r0 + tq, c * LANES:(c + 1) * LANES]
            q2 = jnp.concatenate([jnp.where(low, qp, zero_q), jnp.where(low, zero_q, qp)], axis=0)
            kp = kbuf[n, r0:r0 + tk, ksl]
            vext = jnp.concatenate([vbuf[n, r0:r0 + tk, ksl], ones], axis=1)
            s = lax.dot_general(q2, kp, (((1,), (1,)), ((), ())), preferred_element_type=jnp.float32)
            s_lo = s[:tq] + bias
            s_hi = s[tq:] + bias
            m_lo = jnp.max(s_lo, axis=-1, keepdims=True)
            m_hi = jnp.max(s_hi, axis=-1, keepdims=True)
            if sinks:
                sink_lo = sink_ref[A_HEAD_ORDER[2 * c]] * LOG2E
                sink_hi = sink_ref[A_HEAD_ORDER[2 * c + 1]] * LOG2E
                m_lo = jnp.maximum(m_lo, sink_lo)
                m_hi = jnp.maximum(m_hi, sink_hi)
            p = jnp.concatenate([jnp.exp2(s_lo - m_lo), jnp.exp2(s_hi - m_hi)], axis=0).astype(jnp.bfloat16)
            o2 = jnp.dot(p, vext, preferred_element_type=jnp.float32)
            num = jnp.where(low, o2[:tq, :LANES], o2[tq:, :LANES])
            den = jnp.where(low, o2[:tq, LANES:], o2[tq:, LANES:])
            m = jnp.where(low, m_lo, m_hi)
            if sinks:
                den = den + jnp.exp2(jnp.where(low, sink_lo, sink_hi) - m)
            o_ref[n, r0:r0 + tq, c * LANES:(c + 1) * LANES] = (num / den).astype(o_ref.dtype)
            if emit_lse:
                lse_ref[n, r0:r0 + tq, c * LANES:(c + 1) * LANES] = m + jnp.log(den) * LOG2E


def _const_spec(shape):
    zeros = (0,) * len(shape)
    return pl.BlockSpec(shape, lambda *_: zeros, pipeline_mode=pl.Buffered(1))


def _ffn_in_proj(x3d, pos3d, freq, g1, ffn_w, gm, w_in):
    batch, seq_len, d = x3d.shape
    tm = TOKEN_TILE
    tiles = seq_len // tm
    bf = jnp.bfloat16
    row = lambda w: pl.BlockSpec((None, tm, w), lambda b, i: (b, i, 0))
    dil = lambda dd: pl.BlockSpec((None, dd, tm // dd, B_W), lambda b, i: (b, 0, i, 0))
    nat = lambda w, dt: jax.ShapeDtypeStruct((batch, seq_len, w), dt)
    dil_shape = lambda dd: jax.ShapeDtypeStruct((batch, dd, seq_len // dd, B_W), bf)
    out_specs = [row(d), row(A_Q_W), row(A_KV_W), row(A_KV_W), row(B_W), row(B_W), row(B_W)]
    out_shape = [nat(d, jnp.float32), nat(A_Q_W, bf), nat(A_KV_W, bf), nat(A_KV_W, bf),
                 nat(B_W, bf), nat(B_W, bf), nat(B_W, bf)]
    for _ in range(3):
        out_specs += [dil(dd) for dd in DILATIONS]
        out_shape += [dil_shape(dd) for dd in DILATIONS]
    return pl.pallas_call(
        _ffn_in_proj_kernel,
        grid=(batch, tiles),
        in_specs=[row(d), row(1), _const_spec(freq.shape), _const_spec(g1.shape),
                  *[_const_spec(w.shape) for w in ffn_w], _const_spec(gm.shape), _const_spec(w_in.shape)],
        out_specs=out_specs,
        out_shape=out_shape,
        scratch_shapes=[pltpu.VMEM((tm, d), jnp.bfloat16), pltpu.VMEM((tm, d), jnp.float32),
                        pltpu.VMEM((B_W // LANES, tm, LANES), jnp.float32),
                        pltpu.VMEM((DILATIONS[0] * B_W // LANES, tm // DILATIONS[0], LANES), jnp.float32)],
        compiler_params=pltpu.CompilerParams(dimension_semantics=("arbitrary", "arbitrary"),
                                             vmem_limit_bytes=VMEM_LIMIT),
        name="ffn_in_proj",
    )(x3d, pos3d, freq, g1, *ffn_w, gm, w_in)


def _out_ffn(x1, a, b_parts, woa, wob, g2, ffn_w, gf, final_norm):
    batch, seq_len, d = x1.shape
    tm = TOKEN_TILE
    tiles = seq_len // tm
    row = lambda w: pl.BlockSpec((None, tm, w), lambda b, i: (b, i, 0))
    dil = lambda dd: pl.BlockSpec((None, dd, tm // dd, B_W), lambda b, i: (b, 0, i, 0))
    (o1, l1), (o4, l4), (o16, l16) = b_parts
    d4, d16 = DILATIONS
    slab = lambda: pltpu.VMEM((B_W // LANES, tm, LANES), jnp.float32)
    return pl.pallas_call(
        functools.partial(_out_ffn_kernel, final_norm=final_norm),
        grid=(batch, tiles),
        in_specs=[row(d), row(A_Q_W), row(B_W), row(B_W), dil(d4), dil(d4), dil(d16), dil(d16),
                  _const_spec(woa.shape), _const_spec(wob.shape), _const_spec(g2.shape),
                  *[_const_spec(w.shape) for w in ffn_w], _const_spec(gf.shape)],
        out_specs=row(d),
        out_shape=jax.ShapeDtypeStruct((batch, seq_len, d), jnp.float32),
        scratch_shapes=[pltpu.VMEM((tm, d), jnp.bfloat16), pltpu.VMEM((tm, d), jnp.float32),
                        slab(), slab(), slab(), slab()],
        compiler_params=pltpu.CompilerParams(dimension_semantics=("arbitrary", "arbitrary"),
                                             vmem_limit_bytes=VMEM_LIMIT),
        name="out_ffn",
    )(x1, a, o1, l1, o4, l4, o16, l16, woa, wob, g2, *ffn_w, gf)


def _banded_attention(q, k, v, sink, *, tq, halo, step_rows, emit_lse, name):
    batch, groups, seq_len, wq = q.shape
    wkv = k.shape[-1]
    rows = min(seq_len, step_rows)
    n_seqs = step_rows // rows
    steps = seq_len // rows
    ratio = rows // halo
    halo_blocks = seq_len // halo
    tile = lambda w: pl.BlockSpec((None, n_seqs, rows, w), lambda b, r, i: (b, r, i, 0))
    prev = pl.BlockSpec((None, n_seqs, halo, wkv), lambda b, r, i: (b, r, jnp.maximum(i * ratio - 1, 0), 0))
    nxt = pl.BlockSpec((None, n_seqs, halo, wkv),
                       lambda b, r, i: (b, r, jnp.minimum((i + 1) * ratio, halo_blocks - 1), 0))
    kv_specs = [prev, tile(wkv), nxt]
    in_specs = [tile(wq), *kv_specs, *kv_specs]
    args = [q, k, k, k, v, v, v]
    if sink is not None:
        in_specs = [pl.BlockSpec(memory_space=pltpu.SMEM)] + in_specs
        args = [sink] + args
    out_specs = [tile(wq)]
    out_shape = [jax.ShapeDtypeStruct(q.shape, jnp.bfloat16)]
    if emit_lse:
        out_specs.append(tile(wq))
        out_shape.append(jax.ShapeDtypeStruct(q.shape, jnp.float32))
    tk = tq + 2 * halo
    res = pl.pallas_call(
        functools.partial(_attn_kernel, seq_len=seq_len, tq=tq, halo=halo, sinks=sink is not None,
                          emit_lse=emit_lse),
        grid=(batch, groups // n_seqs, steps),
        in_specs=in_specs,
        out_specs=out_specs,
        out_shape=out_shape,
        scratch_shapes=[pltpu.VMEM((n_seqs, rows + 2 * halo, wkv), jnp.bfloat16),
                        pltpu.VMEM((n_seqs, rows + 2 * halo, wkv), jnp.bfloat16),
                        pltpu.VMEM((3, tq, tk), jnp.float32)],
        compiler_params=pltpu.CompilerParams(dimension_semantics=("arbitrary",) * 3,
                                             vmem_limit_bytes=VMEM_LIMIT),
        name=name,
    )(*args)
    return res


def kernel(x, positions, norm_ffn1, w_gate1, w_up1, w_down1, norm_mix, w_in, a_sink, w_out,
           norm_ffn2, w_gate2, w_up2, w_down2, norm_final):
    batch, seq_len, d = x.shape
    depth = norm_ffn1.shape[0]
    bf = jnp.bfloat16

    inv_freq = 1.0 / (ROPE_THETA ** (jnp.arange(0, HEAD_DIM, 2, dtype=jnp.float32) / HEAD_DIM))
    freq = jnp.tile(inv_freq, LANES // (HEAD_DIM // 2))[None, :]
    pos3d = positions[:, :, None]
    order = jnp.array(A_HEAD_ORDER)
    xc = x

    for l in range(depth):
        ffn1 = tuple(w[l].astype(bf) for w in (w_gate1, w_up1, w_down1))
        ffn2 = tuple(w[l].astype(bf) for w in (w_gate2, w_up2, w_down2))
        w_in_l = w_in[l]
        w_aq = w_in_l[:, :A_Q_W].reshape(d, A_Q_HEADS, HEAD_DIM)[:, order].reshape(d, A_Q_W)
        w_in_l = jnp.concatenate([w_aq, w_in_l[:, A_Q_W:]], axis=1).astype(bf)
        w_out_l = w_out[l]
        woa = w_out_l[:A_Q_W].reshape(A_Q_HEADS, HEAD_DIM, d)[order].reshape(A_Q_W, d).astype(bf)
        wob = w_out_l[A_Q_W:].astype(bf)

        outs = _ffn_in_proj(xc, pos3d, freq, norm_ffn1[l][None, :], ffn1, norm_mix[l][None, :], w_in_l)
        x1, aq, ak, av, bq, bk, bv = outs[:7]
        n_dil = len(DILATIONS)
        bq_d, bk_d, bv_d = outs[7:7 + n_dil], outs[7 + n_dil:7 + 2 * n_dil], outs[7 + 2 * n_dil:]

        (a_out,) = _banded_attention(aq[:, None], ak[:, None], av[:, None], a_sink[l],
                                     tq=2 * A_HALF_WINDOW, halo=A_HALF_WINDOW, step_rows=ATTN_A_STEP_ROWS,
                                     emit_lse=False, name="attn_a")
        b_parts = []
        for window, dilation in B_PATTERNS:
            if dilation == 1:
                qkv = (bq[:, None], bk[:, None], bv[:, None])
            else:
                j = DILATIONS.index(dilation)
                qkv = (bq_d[j], bk_d[j], bv_d[j])
            halo = window // (2 * dilation)
            o, lse = _banded_attention(*qkv, None, tq=2 * halo, halo=halo, step_rows=ATTN_B_STEP_ROWS,
                                       emit_lse=True, name=f"attn_b_d{dilation}")
            if dilation == 1:
                o, lse = o[:, 0], lse[:, 0]
            b_parts.append((o, lse))

        xc = _out_ffn(x1, a_out[:, 0], b_parts, woa, wob, norm_ffn2[l][None, :], ffn2,
                      norm_final[None, :], final_norm=l == depth - 1)
    return xc
```

```python
import functools

import jax
import jax.numpy as jnp
from jax import lax
from jax.experimental import pallas as pl
from jax.experimental.pallas import tpu as pltpu

HEAD_DIM = 64
A_Q_HEADS = 8
A_KV_HEADS = 2
A_HALF_WINDOW = 128
B_HEADS = 8
B_PATTERNS = ((128, 1), (512, 4), (2048, 16))
ROPE_THETA = 10000.0
NORM_EPS = 1e-6
FFN_RES_WEIGHT = 0.5

A_Q_W = A_Q_HEADS * HEAD_DIM
A_KV_W = A_KV_HEADS * HEAD_DIM
B_W = B_HEADS * HEAD_DIM

LANES = 128
NEG_BIG = -1e30
LOG2E = 1.4426950408889634
TOKEN_TILE = 512
FF_CHUNK = 256
ATTN_A_STEP_ROWS = 1024
ATTN_B_STEP_ROWS = 2048
VMEM_LIMIT = 56 * 1024 * 1024
A_HEAD_ORDER = (0, 4, 1, 5, 2, 6, 3, 7)
DILATIONS = tuple(d for _, d in B_PATTERNS if d > 1)
ROPE_GROUPS = LANES // (HEAD_DIM // 2)


def _rms_norm(x, g):
    y = x * lax.rsqrt(jnp.mean(x * x, axis=-1, keepdims=True) + NORM_EPS)
    return y * g


def _swiglu_accumulate(h_ref, wg_ref, wu_ref, wd_ref, acc_ref, before_chunk=None):
    d_ff = wg_ref.shape[1]
    for c in range(d_ff // FF_CHUNK):
        if before_chunk and c in before_chunk:
            before_chunk[c]()
        sl = slice(c * FF_CHUNK, (c + 1) * FF_CHUNK)
        g = jnp.dot(h_ref[...], wg_ref[:, sl], preferred_element_type=jnp.float32)
        u = jnp.dot(h_ref[...], wu_ref[:, sl], preferred_element_type=jnp.float32)
        a = (g * jax.nn.sigmoid(g) * u).astype(jnp.bfloat16)
        down = jnp.dot(a, wd_ref[sl, :], preferred_element_type=jnp.float32)
        if c == 0:
            acc_ref[...] = down
        else:
            acc_ref[...] += down


def _rope(t, cos, sin_signed, first_half):
    parts = []
    for j in range(t.shape[1] // LANES):
        tj = t[:, j * LANES:(j + 1) * LANES]
        partner = jnp.where(first_half, pltpu.roll(tj, LANES - HEAD_DIM // 2, 1), pltpu.roll(tj, HEAD_DIM // 2, 1))
        parts.append(tj * cos + partner * sin_signed)
    return parts[0] if len(parts) == 1 else jnp.concatenate(parts, axis=1)


def _write_deinterleaved(val, slab_ref, stage_ref, out_refs):
    tm, width = val.shape
    n_slabs = width // LANES
    d4, d16 = DILATIONS
    ratio = d16 // d4
    rows4 = tm // d4
    for c in range(n_slabs):
        slab_ref[c] = val[:, c * LANES:(c + 1) * LANES]
    for r in range(d4):
        for c in range(n_slabs):
            part = slab_ref[c, pl.ds(r, rows4, stride=d4), :]
            out_refs[d4][r, :, c * LANES:(c + 1) * LANES] = part.astype(out_refs[d4].dtype)
            stage_ref[r * n_slabs + c] = part
    for r in range(d4):
        for q in range(ratio):
            for c in range(n_slabs):
                part = stage_ref[r * n_slabs + c, pl.ds(q, rows4 // ratio, stride=ratio), :]
                out_refs[d16][q * d4 + r, :, c * LANES:(c + 1) * LANES] = part.astype(out_refs[d16].dtype)


def _ffn_in_proj_kernel(x_ref, pos_ref, freq_ref, g1_ref, wg_ref, wu_ref, wd_ref, gm_ref, win_ref,
                        x1_ref, aq_ref, ak_ref, av_ref, bq_ref, bk_ref, bv_ref, *rest):
    n_dil = len(DILATIONS)
    bq_d = dict(zip(DILATIONS, rest[0:n_dil]))
    bk_d = dict(zip(DILATIONS, rest[n_dil:2 * n_dil]))
    bv_d = dict(zip(DILATIONS, rest[2 * n_dil:3 * n_dil]))
    h_ref, h2_cur_ref, h2_next_ref, acc_ref, slab_ref, stage_ref, cos_ref, sin_ref = rest[3 * n_dil:]

    @pl.when(pl.program_id(0) == 0)
    def _():
        h2_next_ref[...] = jnp.zeros_like(h2_next_ref)

    h2_cur_ref[...] = h2_next_ref[...]

    half = HEAD_DIM // 2
    tm = cos_ref.shape[0]
    ang = pos_ref[...].astype(jnp.float32) * freq_ref[...]
    lane_d = lax.broadcasted_iota(jnp.int32, ang.shape, 1)
    first_half_d = (lane_d % HEAD_DIM) < half
    tables = ((jnp.cos(ang), cos_ref, False), (jnp.sin(ang), sin_ref, True))
    for k in range(ROPE_GROUPS):
        in_group = (lane_d // half) == k
        for table, ref, signed in tables:
            one = jnp.where(in_group, table, 0.0)
            full = one + pltpu.roll(one, half, 1)
            full = full + pltpu.roll(full, 2 * half, 1)
            if signed:
                full = jnp.where(first_half_d, -full, full)
            ref[pl.ds(k, tm // ROPE_GROUPS, stride=ROPE_GROUPS), :] = full
    cos = cos_ref[...]
    sin_signed = sin_ref[...]
    lane = lax.broadcasted_iota(jnp.int32, cos.shape, 1)
    first_half = (lane % HEAD_DIM) < half
    scale = HEAD_DIM ** -0.5 * LOG2E

    def proj(src_ref, lo, width):
        return jnp.dot(src_ref[...], win_ref[:, lo:lo + width], preferred_element_type=jnp.float32)

    offs = {}
    o = 0
    for name, width in (("aq", A_Q_W), ("ak", A_KV_W), ("av", A_KV_W), ("bq", B_W), ("bk", B_W), ("bv", B_W)):
        offs[name] = (o, width)
        o += width
    rope = lambda t: _rope(t, cos, sin_signed, first_half)

    def emit_bv():
        bv = proj(h2_next_ref, *offs["bv"])
        bv_ref[...] = bv.astype(bv_ref.dtype)
        _write_deinterleaved(bv, slab_ref, stage_ref, bv_d)

    def emit_bk():
        bk = rope(proj(h2_next_ref, *offs["bk"]))
        bk_ref[...] = bk.astype(bk_ref.dtype)
        _write_deinterleaved(bk, slab_ref, stage_ref, bk_d)

    def emit_bq():
        bq = rope(proj(h2_next_ref, *offs["bq"])) * scale
        bq_ref[...] = bq.astype(bq_ref.dtype)
        _write_deinterleaved(bq, slab_ref, stage_ref, bq_d)

    x = x_ref[...]
    h_ref[...] = _rms_norm(x, g1_ref[...]).astype(jnp.bfloat16)
    _swiglu_accumulate(h_ref, wg_ref, wu_ref, wd_ref, acc_ref, before_chunk={0: emit_bv, 3: emit_bk, 6: emit_bq})
    x1 = x + FFN_RES_WEIGHT * acc_ref[...]
    x1_ref[...] = x1
    h2_next_ref[...] = _rms_norm(x1, gm_ref[...]).astype(jnp.bfloat16)

    aq_ref[...] = (rope(proj(h2_cur_ref, *offs["aq"])) * scale).astype(aq_ref.dtype)
    ak_ref[...] = rope(proj(h2_cur_ref, *offs["ak"])).astype(ak_ref.dtype)
    av_ref[...] = proj(h2_cur_ref, *offs["av"]).astype(av_ref.dtype)


def _interleave_into(slab_ref, blk_ref, d):
    rows = blk_ref.shape[1]
    for r in range(d):
        for c in range(slab_ref.shape[0]):
            slab_ref[c, pl.ds(r, rows, stride=d), :] = blk_ref[r, :, c * LANES:(c + 1) * LANES].astype(jnp.float32)


def _out_ffn_kernel(x1_ref, a_ref, o1_ref, l1_ref, o4_ref, l4_ref, o16_ref, l16_ref,
                    woa_ref, wob_ref, g2_ref, wg_ref, wu_ref, wd_ref, gf_ref,
                    out_ref, h_ref, acc_ref, so4_ref, sl4_ref, so16_ref, sl16_ref, *, final_norm):
    d4, d16 = DILATIONS
    _interleave_into(so4_ref, o4_ref, d4)
    _interleave_into(sl4_ref, l4_ref, d4)
    _interleave_into(so16_ref, o16_ref, d16)
    _interleave_into(sl16_ref, l16_ref, d16)

    mix = jnp.dot(a_ref[...], woa_ref[...], preferred_element_type=jnp.float32)
    for c in range(B_W // LANES):
        sl = slice(c * LANES, (c + 1) * LANES)
        la, lb, lc = l1_ref[:, sl], sl4_ref[c], sl16_ref[c]
        m = jnp.maximum(jnp.maximum(la, lb), lc)
        wa, wb, wc = jnp.exp2(la - m), jnp.exp2(lb - m), jnp.exp2(lc - m)
        num = wa * o1_ref[:, sl].astype(jnp.float32) + wb * so4_ref[c] + wc * so16_ref[c]
        b = (num / (wa + wb + wc)).astype(jnp.bfloat16)
        mix += jnp.dot(b, wob_ref[sl, :], preferred_element_type=jnp.float32)
    x2 = x1_ref[...] + mix
    h_ref[...] = _rms_norm(x2, g2_ref[...]).astype(jnp.bfloat16)
    _swiglu_accumulate(h_ref, wg_ref, wu_ref, wd_ref, acc_ref)
    x3 = x2 + FFN_RES_WEIGHT * acc_ref[...]
    out_ref[...] = _rms_norm(x3, gf_ref[...]) if final_norm else x3


def _attn_kernel(*refs, seq_len, tq, halo, sinks, emit_lse):
    if sinks:
        sink_ref, refs = refs[0], refs[1:]
    q_ref, kp_ref, kc_ref, kn_ref, vp_ref, vc_ref, vn_ref, o_ref = refs[:8]
    refs = refs[8:]
    if emit_lse:
        lse_ref, refs = refs[0], refs[1:]
    kbuf, vbuf, bias_ref = refs

    n_seqs, rows = q_ref.shape[:2]
    tk = tq + 2 * halo
    tiles = rows // tq
    tiles_in_seq = seq_len // tq
    kv_tiles = kbuf.shape[2] // LANES
    q_tiles = q_ref.shape[2] // LANES
    step = pl.program_id(2)

    @pl.when((pl.program_id(0) == 0) & (pl.program_id(1) == 0) & (step == 0))
    def _():
        row = lax.broadcasted_iota(jnp.int32, (tq, tk), 0)
        col = lax.broadcasted_iota(jnp.int32, (tq, tk), 1)
        band = jnp.abs(row + halo - col) <= halo
        bias_ref[0] = jnp.where(band & (col >= halo), 0.0, NEG_BIG)
        bias_ref[1] = jnp.where(band, 0.0, NEG_BIG)
        bias_ref[2] = jnp.where(band & (col < tq + halo), 0.0, NEG_BIG)

    kbuf[:, 0:halo, :] = kp_ref[...]
    kbuf[:, halo:halo + rows, :] = kc_ref[...]
    kbuf[:, halo + rows:, :] = kn_ref[...]
    vbuf[:, 0:halo, :] = vp_ref[...]
    vbuf[:, halo:halo + rows, :] = vc_ref[...]
    vbuf[:, halo + rows:, :] = vn_ref[...]

    lane = lax.broadcasted_iota(jnp.int32, (tq, LANES), 1)
    low = lane < HEAD_DIM
    ones = jnp.ones((tk, LANES), jnp.bfloat16)
    zero_q = jnp.zeros((tq, LANES), jnp.bfloat16)

    for n, t in [(n, t) for n in range(n_seqs) for t in range(tiles)]:
        g = step * tiles + t
        case = jnp.where(g == 0, 0, jnp.where(g == tiles_in_seq - 1, 2, 1))
        bias = bias_ref[case]
        r0 = t * tq
        for c in range(q_tiles):
            kc = c if kv_tiles == q_tiles else 0
            ksl = slice(kc * LANES, (kc + 1) * LANES)
            qp = q_ref[n, r0:r0 + tq, c * LANES:(c + 1) * LANES]
            q2 = jnp.concatenate([jnp.where(low, qp, zero_q), jnp.where(low, zero_q, qp)], axis=0)
            kp = kbuf[n, r0:r0 + tk, ksl]
            vext = jnp.concatenate([vbuf[n, r0:r0 + tk, ksl], ones], axis=1)
            s = lax.dot_general(q2, kp, (((1,), (1,)), ((), ())), preferred_element_type=jnp.float32)
            s_lo = s[:tq] + bias
            s_hi = s[tq:] + bias
            m_lo = jnp.max(s_lo, axis=-1, keepdims=True)
            m_hi = jnp.max(s_hi, axis=-1, keepdims=True)
            if sinks:
                sink_lo = sink_ref[A_HEAD_ORDER[2 * c]] * LOG2E
                sink_hi = sink_ref[A_HEAD_ORDER[2 * c + 1]] * LOG2E
                m_lo = jnp.maximum(m_lo, sink_lo)
                m_hi = jnp.maximum(m_hi, sink_hi)
            p = jnp.concatenate([jnp.exp2(s_lo - m_lo), jnp.exp2(s_hi - m_hi)], axis=0).astype(jnp.bfloat16)
            o2 = jnp.dot(p, vext, preferred_element_type=jnp.float32)
            num = jnp.where(low, o2[:tq, :LANES], o2[tq:, :LANES])
            den = jnp.where(low, o2[:tq, LANES:], o2[tq:, LANES:])
            m = jnp.where(low, m_lo, m_hi)
            if sinks:
                den = den + jnp.exp2(jnp.where(low, sink_lo, sink_hi) - m)
            o_ref[n, r0:r0 + tq, c * LANES:(c + 1) * LANES] = (num / den).astype(o_ref.dtype)
            if emit_lse:
                lse_ref[n, r0:r0 + tq, c * LANES:(c + 1) * LANES] = m + jnp.log(den) * LOG2E


def _const_spec(shape):
    zeros = (0,) * len(shape)
    return pl.BlockSpec(shape, lambda *_: zeros, pipeline_mode=pl.Buffered(1))


def _ffn_in_proj(x3d, pos3d, freq, g1, ffn_w, gm, w_in):
    batch, seq_len, d = x3d.shape
    tm = TOKEN_TILE
    tiles = seq_len // tm
    bf = jnp.bfloat16
    n_tiles = batch * tiles
    ffn_tile = lambda t: jnp.minimum(t, n_tiles - 1)
    proj_tile = lambda t: jnp.maximum(t - 1, 0)
    row = lambda w, which: pl.BlockSpec((None, tm, w), lambda t: (which(t) // tiles, which(t) % tiles, 0))
    dil = lambda dd: pl.BlockSpec((None, dd, tm // dd, B_W),
                                  lambda t: (proj_tile(t) // tiles, 0, proj_tile(t) % tiles, 0))
    nat = lambda w, dt: jax.ShapeDtypeStruct((batch, seq_len, w), dt)
    dil_shape = lambda dd: jax.ShapeDtypeStruct((batch, dd, seq_len // dd, B_W), bf)
    out_specs = [row(d, ffn_tile)] + [row(w, proj_tile) for w in (A_Q_W, A_KV_W, A_KV_W, B_W, B_W, B_W)]
    out_shape = [nat(d, jnp.float32), nat(A_Q_W, bf), nat(A_KV_W, bf), nat(A_KV_W, bf),
                 nat(B_W, bf), nat(B_W, bf), nat(B_W, bf)]
    for _ in range(3):
        out_specs += [dil(dd) for dd in DILATIONS]
        out_shape += [dil_shape(dd) for dd in DILATIONS]
    return pl.pallas_call(
        _ffn_in_proj_kernel,
        grid=(n_tiles + 1,),
        in_specs=[row(d, ffn_tile),
                  pl.BlockSpec((None, tm // ROPE_GROUPS, LANES),
                               lambda t: (proj_tile(t) // tiles, proj_tile(t) % tiles, 0)),
                  _const_spec(freq.shape), _const_spec(g1.shape),
                  *[_const_spec(w.shape) for w in ffn_w], _const_spec(gm.shape), _const_spec(w_in.shape)],
        out_specs=out_specs,
        out_shape=out_shape,
        scratch_shapes=[pltpu.VMEM((tm, d), jnp.bfloat16), pltpu.VMEM((tm, d), jnp.bfloat16),
                        pltpu.VMEM((tm, d), jnp.bfloat16), pltpu.VMEM((tm, d), jnp.float32),
                        pltpu.VMEM((B_W // LANES, tm, LANES), jnp.float32),
                        pltpu.VMEM((DILATIONS[0] * B_W // LANES, tm // DILATIONS[0], LANES), jnp.float32),
                        pltpu.VMEM((tm, LANES), jnp.float32), pltpu.VMEM((tm, LANES), jnp.float32)],
        compiler_params=pltpu.CompilerParams(dimension_semantics=("arbitrary",),
                                             vmem_limit_bytes=VMEM_LIMIT),
        name="ffn_in_proj",
    )(x3d, pos3d, freq, g1, *ffn_w, gm, w_in)


def _out_ffn(x1, a, b_parts, woa, wob, g2, ffn_w, gf, final_norm):
    batch, seq_len, d = x1.shape
    tm = TOKEN_TILE
    tiles = seq_len // tm
    row = lambda w: pl.BlockSpec((None, tm, w), lambda b, i: (b, i, 0))
    dil = lambda dd: pl.BlockSpec((None, dd, tm // dd, B_W), lambda b, i: (b, 0, i, 0))
    (o1, l1), (o4, l4), (o16, l16) = b_parts
    d4, d16 = DILATIONS
    slab = lambda: pltpu.VMEM((B_W // LANES, tm, LANES), jnp.float32)
    return pl.pallas_call(
        functools.partial(_out_ffn_kernel, final_norm=final_norm),
        grid=(batch, tiles),
        in_specs=[row(d), row(A_Q_W), row(B_W), row(B_W), dil(d4), dil(d4), dil(d16), dil(d16),
                  _const_spec(woa.shape), _const_spec(wob.shape), _const_spec(g2.shape),
                  *[_const_spec(w.shape) for w in ffn_w], _const_spec(gf.shape)],
        out_specs=row(d),
        out_shape=jax.ShapeDtypeStruct((batch, seq_len, d), jnp.float32),
        scratch_shapes=[pltpu.VMEM((tm, d), jnp.bfloat16), pltpu.VMEM((tm, d), jnp.float32),
                        slab(), slab(), slab(), slab()],
        compiler_params=pltpu.CompilerParams(dimension_semantics=("arbitrary", "arbitrary"),
                                             vmem_limit_bytes=VMEM_LIMIT),
        name="out_ffn",
    )(x1, a, o1, l1, o4, l4, o16, l16, woa, wob, g2, *ffn_w, gf)


def _banded_attention(q, k, v, sink, *, tq, halo, step_rows, emit_lse, name):
    batch, groups, seq_len, wq = q.shape
    wkv = k.shape[-1]
    rows = min(seq_len, step_rows)
    n_seqs = step_rows // rows
    steps = seq_len // rows
    ratio = rows // halo
    halo_blocks = seq_len // halo
    tile = lambda w: pl.BlockSpec((None, n_seqs, rows, w), lambda b, r, i: (b, r, i, 0))
    prev = pl.BlockSpec((None, n_seqs, halo, wkv), lambda b, r, i: (b, r, jnp.maximum(i * ratio - 1, 0), 0))
    nxt = pl.BlockSpec((None, n_seqs, halo, wkv),
                       lambda b, r, i: (b, r, jnp.minimum((i + 1) * ratio, halo_blocks - 1), 0))
    kv_specs = [prev, tile(wkv), nxt]
    in_specs = [tile(wq), *kv_specs, *kv_specs]
    args = [q, k, k, k, v, v, v]
    if sink is not None:
        in_specs = [pl.BlockSpec(memory_space=pltpu.SMEM)] + in_specs
        args = [sink] + args
    out_specs = [tile(wq)]
    out_shape = [jax.ShapeDtypeStruct(q.shape, jnp.bfloat16)]
    if emit_lse:
        out_specs.append(tile(wq))
        out_shape.append(jax.ShapeDtypeStruct(q.shape, jnp.float32))
    tk = tq + 2 * halo
    res = pl.pallas_call(
        functools.partial(_attn_kernel, seq_len=seq_len, tq=tq, halo=halo, sinks=sink is not None,
                          emit_lse=emit_lse),
        grid=(batch, groups // n_seqs, steps),
        in_specs=in_specs,
        out_specs=out_specs,
        out_shape=out_shape,
        scratch_shapes=[pltpu.VMEM((n_seqs, rows + 2 * halo, wkv), jnp.bfloat16),
                        pltpu.VMEM((n_seqs, rows + 2 * halo, wkv), jnp.bfloat16),
                        pltpu.VMEM((3, tq, tk), jnp.float32)],
        compiler_params=pltpu.CompilerParams(dimension_semantics=("arbitrary",) * 3,
                                             vmem_limit_bytes=VMEM_LIMIT),
        name=name,
    )(*args)
    return res


def kernel(x, positions, norm_ffn1, w_gate1, w_up1, w_down1, norm_mix, w_in, a_sink, w_out,
           norm_ffn2, w_gate2, w_up2, w_down2, norm_final):
    batch, seq_len, d = x.shape
    depth = norm_ffn1.shape[0]
    bf = jnp.bfloat16

    inv_freq = 1.0 / (ROPE_THETA ** (jnp.arange(0, HEAD_DIM, 2, dtype=jnp.float32) / HEAD_DIM))
    freq = jnp.tile(inv_freq, LANES // (HEAD_DIM // 2))[None, :]
    pos3d = jnp.repeat(positions.reshape(batch, seq_len // ROPE_GROUPS, ROPE_GROUPS), HEAD_DIM // 2, axis=-1)
    order = jnp.array(A_HEAD_ORDER)
    xc = x

    for l in range(depth):
        ffn1 = tuple(w[l].astype(bf) for w in (w_gate1, w_up1, w_down1))
        ffn2 = tuple(w[l].astype(bf) for w in (w_gate2, w_up2, w_down2))
        w_in_l = w_in[l]
        w_aq = w_in_l[:, :A_Q_W].reshape(d, A_Q_HEADS, HEAD_DIM)[:, order].reshape(d, A_Q_W)
        w_in_l = jnp.concatenate([w_aq, w_in_l[:, A_Q_W:]], axis=1).astype(bf)
        w_out_l = w_out[l]
        woa = w_out_l[:A_Q_W].reshape(A_Q_HEADS, HEAD_DIM, d)[order].reshape(A_Q_W, d).astype(bf)
        wob = w_out_l[A_Q_W:].astype(bf)

        outs = _ffn_in_proj(xc, pos3d, freq, norm_ffn1[l][None, :], ffn1, norm_mix[l][None, :], w_in_l)
        x1, aq, ak, av, bq, bk, bv = outs[:7]
        n_dil = len(DILATIONS)
        bq_d, bk_d, bv_d = outs[7:7 + n_dil], outs[7 + n_dil:7 + 2 * n_dil], outs[7 + 2 * n_dil:]

        (a_out,) = _banded_attention(aq[:, None], ak[:, None], av[:, None], a_sink[l],
                                     tq=2 * A_HALF_WINDOW, halo=A_HALF_WINDOW, step_rows=ATTN_A_STEP_ROWS,
                                     emit_lse=False, name="attn_a")
        b_parts = []
        for window, dilation in B_PATTERNS:
            if dilation == 1:
                qkv = (bq[:, None], bk[:, None], bv[:, None])
            else:
                j = DILATIONS.index(dilation)
                qkv = (bq_d[j], bk_d[j], bv_d[j])
            halo = window // (2 * dilation)
            o, lse = _banded_attention(*qkv, None, tq=2 * halo, halo=halo, step_rows=ATTN_B_STEP_ROWS,
                                       emit_lse=True, name=f"attn_b_d{dilation}")
            if dilation == 1:
                o, lse = o[:, 0], lse[:, 0]
            b_parts.append((o, lse))

        xc = _out_ffn(x1, a_out[:, 0], b_parts, woa, wob, norm_ffn2[l][None, :], ffn2,
                      norm_final[None, :], final_norm=l == depth - 1)
    return xc
```

```python
import functools

import jax
import jax.numpy as jnp
from jax import lax
from jax.experimental import pallas as pl
from jax.experimental.pallas import tpu as pltpu

HEAD_DIM = 64
A_Q_HEADS = 8
A_KV_HEADS = 2
A_HALF_WINDOW = 128
B_HEADS = 8
B_PATTERNS = ((128, 1), (512, 4), (2048, 16))
ROPE_THETA = 10000.0
NORM_EPS = 1e-6
FFN_RES_WEIGHT = 0.5

A_Q_W = A_Q_HEADS * HEAD_DIM
A_KV_W = A_KV_HEADS * HEAD_DIM
B_W = B_HEADS * HEAD_DIM

LANES = 128
NEG_BIG = -1e30
LOG2E = 1.4426950408889634
TOKEN_TILE = 512
FF_CHUNK = 256
ATTN_STEP_ROWS = 2048
VMEM_LIMIT = 56 * 1024 * 1024
A_HEAD_ORDER = (0, 4, 1, 5, 2, 6, 3, 7)
DILATIONS = tuple(d for _, d in B_PATTERNS if d > 1)
ROPE_GROUPS = LANES // (HEAD_DIM // 2)


def _rms_norm(x, g):
    y = x * lax.rsqrt(jnp.mean(x * x, axis=-1, keepdims=True) + NORM_EPS)
    return y * g


def _swiglu_accumulate(h_ref, wg_ref, wu_ref, wd_ref, acc_ref, before_chunk=None):
    d_ff = wg_ref.shape[1]
    for c in range(d_ff // FF_CHUNK):
        if before_chunk and c in before_chunk:
            before_chunk[c]()
        sl = slice(c * FF_CHUNK, (c + 1) * FF_CHUNK)
        g = jnp.dot(h_ref[...], wg_ref[:, sl], preferred_element_type=jnp.float32)
        u = jnp.dot(h_ref[...], wu_ref[:, sl], preferred_element_type=jnp.float32)
        a = (g * jax.nn.sigmoid(g) * u).astype(jnp.bfloat16)
        down = jnp.dot(a, wd_ref[sl, :], preferred_element_type=jnp.float32)
        if c == 0:
            acc_ref[...] = down
        else:
            acc_ref[...] += down


def _rope(t, cos, sin_signed, first_half):
    parts = []
    for j in range(t.shape[1] // LANES):
        tj = t[:, j * LANES:(j + 1) * LANES]
        partner = jnp.where(first_half, pltpu.roll(tj, LANES - HEAD_DIM // 2, 1), pltpu.roll(tj, HEAD_DIM // 2, 1))
        parts.append(tj * cos + partner * sin_signed)
    return parts[0] if len(parts) == 1 else jnp.concatenate(parts, axis=1)


def _write_deinterleaved(val, slab_ref, stage_ref, out_refs):
    tm, width = val.shape
    n_slabs = width // LANES
    d4, d16 = DILATIONS
    ratio = d16 // d4
    rows4 = tm // d4
    for c in range(n_slabs):
        slab_ref[c] = val[:, c * LANES:(c + 1) * LANES]
    for r in range(d4):
        for c in range(n_slabs):
            part = slab_ref[c, pl.ds(r, rows4, stride=d4), :]
            out_refs[d4][r, :, c * LANES:(c + 1) * LANES] = part.astype(out_refs[d4].dtype)
            stage_ref[r * n_slabs + c] = part
    for r in range(d4):
        for q in range(ratio):
            for c in range(n_slabs):
                part = stage_ref[r * n_slabs + c, pl.ds(q, rows4 // ratio, stride=ratio), :]
                out_refs[d16][q * d4 + r, :, c * LANES:(c + 1) * LANES] = part.astype(out_refs[d16].dtype)


def _ffn_in_proj_kernel(x_ref, pos_ref, freq_ref, g1_ref, wg_ref, wu_ref, wd_ref, gm_ref, win_ref,
                        x1_ref, aq_ref, ak_ref, av_ref, bq_ref, bk_ref, bv_ref, *rest):
    n_dil = len(DILATIONS)
    bq_d = dict(zip(DILATIONS, rest[0:n_dil]))
    bk_d = dict(zip(DILATIONS, rest[n_dil:2 * n_dil]))
    bv_d = dict(zip(DILATIONS, rest[2 * n_dil:3 * n_dil]))
    h_ref, h2_cur_ref, h2_next_ref, acc_ref, slab_ref, stage_ref, cos_ref, sin_ref = rest[3 * n_dil:]

    @pl.when(pl.program_id(0) == 0)
    def _():
        h2_next_ref[...] = jnp.zeros_like(h2_next_ref)

    h2_cur_ref[...] = h2_next_ref[...]

    half = HEAD_DIM // 2
    tm = cos_ref.shape[0]
    ang = pos_ref[...].astype(jnp.float32) * freq_ref[...]
    lane_d = lax.broadcasted_iota(jnp.int32, ang.shape, 1)
    first_half_d = (lane_d % HEAD_DIM) < half
    tables = ((jnp.cos(ang), cos_ref, False), (jnp.sin(ang), sin_ref, True))
    for k in range(ROPE_GROUPS):
        in_group = (lane_d // half) == k
        for table, ref, signed in tables:
            one = jnp.where(in_group, table, 0.0)
            full = one + pltpu.roll(one, half, 1)
            full = full + pltpu.roll(full, 2 * half, 1)
            if signed:
                full = jnp.where(first_half_d, -full, full)
            ref[pl.ds(k, tm // ROPE_GROUPS, stride=ROPE_GROUPS), :] = full
    cos = cos_ref[...]
    sin_signed = sin_ref[...]
    lane = lax.broadcasted_iota(jnp.int32, cos.shape, 1)
    first_half = (lane % HEAD_DIM) < half
    scale = HEAD_DIM ** -0.5 * LOG2E

    def proj(src_ref, lo, width):
        return jnp.dot(src_ref[...], win_ref[:, lo:lo + width], preferred_element_type=jnp.float32)

    offs = {}
    o = 0
    for name, width in (("aq", A_Q_W), ("ak", A_KV_W), ("av", A_KV_W), ("bq", B_W), ("bk", B_W), ("bv", B_W)):
        offs[name] = (o, width)
        o += width
    rope = lambda t: _rope(t, cos, sin_signed, first_half)

    def emit_bv():
        bv = proj(h2_next_ref, *offs["bv"])
        bv_ref[...] = bv.astype(bv_ref.dtype)
        _write_deinterleaved(bv, slab_ref, stage_ref, bv_d)

    def emit_bk():
        bk = rope(proj(h2_next_ref, *offs["bk"]))
        bk_ref[...] = bk.astype(bk_ref.dtype)
        _write_deinterleaved(bk, slab_ref, stage_ref, bk_d)

    def emit_bq():
        bq = rope(proj(h2_next_ref, *offs["bq"])) * scale
        bq_ref[...] = bq.astype(bq_ref.dtype)
        _write_deinterleaved(bq, slab_ref, stage_ref, bq_d)

    x = x_ref[...]
    h_ref[...] = _rms_norm(x, g1_ref[...]).astype(jnp.bfloat16)
    _swiglu_accumulate(h_ref, wg_ref, wu_ref, wd_ref, acc_ref, before_chunk={0: emit_bv, 3: emit_bk, 6: emit_bq})
    x1 = x + acc_ref[...]
    x1_ref[...] = x1
    h2_next_ref[...] = _rms_norm(x1, gm_ref[...]).astype(jnp.bfloat16)

    aq_ref[...] = (rope(proj(h2_cur_ref, *offs["aq"])) * scale).astype(aq_ref.dtype)
    ak_ref[...] = rope(proj(h2_cur_ref, *offs["ak"])).astype(ak_ref.dtype)
    av_ref[...] = proj(h2_cur_ref, *offs["av"]).astype(av_ref.dtype)


def _interleave_into(slab_ref, blk_ref, d, stage_ref=None):
    n_slabs = slab_ref.shape[0]
    rows = blk_ref.shape[1]
    if stage_ref is None:
        for r in range(d):
            for c in range(n_slabs):
                slab_ref[c, pl.ds(r, rows, stride=d), :] = blk_ref[r, :, c * LANES:(c + 1) * LANES].astype(jnp.float32)
        return
    d4 = DILATIONS[0]
    ratio = d // d4
    for r in range(d4):
        for c in range(n_slabs):
            for q in range(ratio):
                part = blk_ref[q * d4 + r, :, c * LANES:(c + 1) * LANES].astype(jnp.float32)
                stage_ref[r * n_slabs + c, pl.ds(q, rows, stride=ratio), :] = part
            slab_ref[c, pl.ds(r, rows * ratio, stride=d4), :] = stage_ref[r * n_slabs + c]


def _out_ffn_kernel(x1_ref, a_ref, o1_ref, l1_ref, o4_ref, l4_ref, o16_ref, l16_ref,
                    woa_ref, wob_ref, g2_ref, wg_ref, wu_ref, wd_ref, gf_ref,
                    out_ref, h_ref, acc_ref, so4_ref, sl4_ref, so16_ref, sl16_ref, stage_o_ref, stage_l_ref,
                    *, final_norm):
    d4, d16 = DILATIONS
    _interleave_into(so4_ref, o4_ref, d4)
    _interleave_into(sl4_ref, l4_ref, d4)
    _interleave_into(so16_ref, o16_ref, d16, stage_o_ref)
    _interleave_into(sl16_ref, l16_ref, d16, stage_l_ref)

    mix = jnp.dot(a_ref[...], woa_ref[...], preferred_element_type=jnp.float32)
    for c in range(B_W // LANES):
        sl = slice(c * LANES, (c + 1) * LANES)
        la, lb, lc = l1_ref[:, sl], sl4_ref[c], sl16_ref[c]
        m = jnp.maximum(jnp.maximum(la, lb), lc)
        wa, wb, wc = jnp.exp2(la - m), jnp.exp2(lb - m), jnp.exp2(lc - m)
        num = wa * o1_ref[:, sl].astype(jnp.float32) + wb * so4_ref[c] + wc * so16_ref[c]
        b = (num / (wa + wb + wc)).astype(jnp.bfloat16)
        mix += jnp.dot(b, wob_ref[sl, :], preferred_element_type=jnp.float32)
    x2 = x1_ref[...] + mix
    h_ref[...] = _rms_norm(x2, g2_ref[...]).astype(jnp.bfloat16)
    _swiglu_accumulate(h_ref, wg_ref, wu_ref, wd_ref, acc_ref)
    x3 = x2 + acc_ref[...]
    out_ref[...] = _rms_norm(x3, gf_ref[...]) if final_norm else x3


def _attn_kernel(*refs, seq_len, tq, halo, sinks, emit_lse):
    if sinks:
        sink_ref, refs = refs[0], refs[1:]
    q_ref, kp_ref, kc_ref, kn_ref, vp_ref, vc_ref, vn_ref, o_ref = refs[:8]
    refs = refs[8:]
    if emit_lse:
        lse_ref, refs = refs[0], refs[1:]
    kbuf, vbuf, bias_ref = refs

    n_seqs, rows = q_ref.shape[:2]
    tk = tq + 2 * halo
    tiles = rows // tq
    tiles_in_seq = seq_len // tq
    kv_tiles = kbuf.shape[2] // LANES
    q_tiles = q_ref.shape[2] // LANES
    step = pl.program_id(2)

    @pl.when((pl.program_id(0) == 0) & (pl.program_id(1) == 0) & (step == 0))
    def _():
        row = lax.broadcasted_iota(jnp.int32, (tq, tk), 0)
        col = lax.broadcasted_iota(jnp.int32, (tq, tk), 1)
        band = jnp.abs(row + halo - col) <= halo
        bias_ref[0] = jnp.where(band & (col >= halo), 0.0, NEG_BIG)
        bias_ref[1] = jnp.where(band, 0.0, NEG_BIG)
        bias_ref[2] = jnp.where(band & (col < tq + halo), 0.0, NEG_BIG)

    kbuf[:, 0:halo, :] = kp_ref[...]
    kbuf[:, halo:halo + rows, :] = kc_ref[...]
    kbuf[:, halo + rows:, :] = kn_ref[...]
    vbuf[:, 0:halo, :] = vp_ref[...]
    vbuf[:, halo:halo + rows, :] = vc_ref[...]
    vbuf[:, halo + rows:, :] = vn_ref[...]

    lane = lax.broadcasted_iota(jnp.int32, (tq, LANES), 1)
    low = lane < HEAD_DIM
    ones = jnp.ones((tk, LANES), jnp.bfloat16)
    zero_q = jnp.zeros((tq, LANES), jnp.bfloat16)

    def scores(n, t, c):
        kc = c if kv_tiles == q_tiles else 0
        r0 = t * tq
        qp = q_ref[n, r0:r0 + tq, c * LANES:(c + 1) * LANES]
        q2 = jnp.concatenate([jnp.where(low, qp, zero_q), jnp.where(low, zero_q, qp)], axis=0)
        kp = kbuf[n, r0:r0 + tk, kc * LANES:(kc + 1) * LANES]
        return lax.dot_general(q2, kp, (((1,), (1,)), ((), ())), preferred_element_type=jnp.float32)

    def finish(n, t, c, s):
        kc = c if kv_tiles == q_tiles else 0
        r0 = t * tq
        g = step * tiles + t
        bias = bias_ref[jnp.where(g == 0, 0, jnp.where(g == tiles_in_seq - 1, 2, 1))]
        vext = jnp.concatenate([vbuf[n, r0:r0 + tk, kc * LANES:(kc + 1) * LANES], ones], axis=1)
        s_lo = s[:tq] + bias
        s_hi = s[tq:] + bias
        m_lo = jnp.max(s_lo, axis=-1, keepdims=True)
        m_hi = jnp.max(s_hi, axis=-1, keepdims=True)
        if sinks:
            sink_lo = sink_ref[A_HEAD_ORDER[2 * c]] * LOG2E
            sink_hi = sink_ref[A_HEAD_ORDER[2 * c + 1]] * LOG2E
            m_lo = jnp.maximum(m_lo, sink_lo)
            m_hi = jnp.maximum(m_hi, sink_hi)
        p = jnp.concatenate([jnp.exp2(s_lo - m_lo), jnp.exp2(s_hi - m_hi)], axis=0).astype(jnp.bfloat16)
        o2 = jnp.dot(p, vext, preferred_element_type=jnp.float32)
        num = jnp.where(low, o2[:tq, :LANES], o2[tq:, :LANES])
        den = jnp.where(low, o2[:tq, LANES:], o2[tq:, LANES:])
        m = jnp.where(low, m_lo, m_hi)
        if sinks:
            den = den + jnp.exp2(jnp.where(low, sink_lo, sink_hi) - m)
        o_ref[n, r0:r0 + tq, c * LANES:(c + 1) * LANES] = (num / den).astype(o_ref.dtype)
        if emit_lse:
            lse_ref[n, r0:r0 + tq, c * LANES:(c + 1) * LANES] = m + jnp.log(den) * LOG2E

    for n, t, c in [(n, t, c) for n in range(n_seqs) for t in range(tiles) for c in range(q_tiles)]:
        finish(n, t, c, scores(n, t, c))


def _const_spec(shape):
    zeros = (0,) * len(shape)
    return pl.BlockSpec(shape, lambda *_: zeros, pipeline_mode=pl.Buffered(1))


def _ffn_in_proj(x3d, pos3d, freq, g1, ffn_w, gm, w_in):
    batch, seq_len, d = x3d.shape
    tm = TOKEN_TILE
    tiles = seq_len // tm
    bf = jnp.bfloat16
    n_tiles = batch * tiles
    ffn_tile = lambda t: jnp.minimum(t, n_tiles - 1)
    proj_tile = lambda t: jnp.maximum(t - 1, 0)
    row = lambda w, which: pl.BlockSpec((None, tm, w), lambda t: (which(t) // tiles, which(t) % tiles, 0))
    dil = lambda dd: pl.BlockSpec((None, dd, tm // dd, B_W),
                                  lambda t: (proj_tile(t) // tiles, 0, proj_tile(t) % tiles, 0))
    nat = lambda w, dt: jax.ShapeDtypeStruct((batch, seq_len, w), dt)
    dil_shape = lambda dd: jax.ShapeDtypeStruct((batch, dd, seq_len // dd, B_W), bf)
    out_specs = [row(d, ffn_tile)] + [row(w, proj_tile) for w in (A_Q_W, A_KV_W, A_KV_W, B_W, B_W, B_W)]
    out_shape = [nat(d, jnp.float32), nat(A_Q_W, bf), nat(A_KV_W, bf), nat(A_KV_W, bf),
                 nat(B_W, bf), nat(B_W, bf), nat(B_W, bf)]
    for _ in range(3):
        out_specs += [dil(dd) for dd in DILATIONS]
        out_shape += [dil_shape(dd) for dd in DILATIONS]
    return pl.pallas_call(
        _ffn_in_proj_kernel,
        grid=(n_tiles + 1,),
        in_specs=[row(d, ffn_tile),
                  pl.BlockSpec((None, tm // ROPE_GROUPS, LANES),
                               lambda t: (proj_tile(t) // tiles, proj_tile(t) % tiles, 0)),
                  _const_spec(freq.shape), _const_spec(g1.shape),
                  *[_const_spec(w.shape) for w in ffn_w], _const_spec(gm.shape), _const_spec(w_in.shape)],
        out_specs=out_specs,
        out_shape=out_shape,
        scratch_shapes=[pltpu.VMEM((tm, d), jnp.bfloat16), pltpu.VMEM((tm, d), jnp.bfloat16),
                        pltpu.VMEM((tm, d), jnp.bfloat16), pltpu.VMEM((tm, d), jnp.float32),
                        pltpu.VMEM((B_W // LANES, tm, LANES), jnp.float32),
                        pltpu.VMEM((DILATIONS[0] * B_W // LANES, tm // DILATIONS[0], LANES), jnp.float32),
                        pltpu.VMEM((tm, LANES), jnp.float32), pltpu.VMEM((tm, LANES), jnp.float32)],
        compiler_params=pltpu.CompilerParams(dimension_semantics=("arbitrary",),
                                             vmem_limit_bytes=VMEM_LIMIT),
        name="ffn_in_proj",
    )(x3d, pos3d, freq, g1, *ffn_w, gm, w_in)


def _out_ffn(x1, a, b_parts, woa, wob, g2, ffn_w, gf, final_norm):
    batch, seq_len, d = x1.shape
    tm = TOKEN_TILE
    tiles = seq_len // tm
    row = lambda w: pl.BlockSpec((None, tm, w), lambda b, i: (b, i, 0))
    dil = lambda dd: pl.BlockSpec((None, dd, tm // dd, B_W), lambda b, i: (b, 0, i, 0))
    (o1, l1), (o4, l4), (o16, l16) = b_parts
    d4, d16 = DILATIONS
    slab = lambda: pltpu.VMEM((B_W // LANES, tm, LANES), jnp.float32)
    stage = lambda: pltpu.VMEM((d4 * B_W // LANES, tm // d4, LANES), jnp.float32)
    return pl.pallas_call(
        functools.partial(_out_ffn_kernel, final_norm=final_norm),
        grid=(batch, tiles),
        in_specs=[row(d), row(A_Q_W), row(B_W), row(B_W), dil(d4), dil(d4), dil(d16), dil(d16),
                  _const_spec(woa.shape), _const_spec(wob.shape), _const_spec(g2.shape),
                  *[_const_spec(w.shape) for w in ffn_w], _const_spec(gf.shape)],
        out_specs=row(d),
        out_shape=jax.ShapeDtypeStruct((batch, seq_len, d), jnp.float32),
        scratch_shapes=[pltpu.VMEM((tm, d), jnp.bfloat16), pltpu.VMEM((tm, d), jnp.float32),
                        slab(), slab(), slab(), slab(), stage(), stage()],
        compiler_params=pltpu.CompilerParams(dimension_semantics=("arbitrary", "arbitrary"),
                                             vmem_limit_bytes=VMEM_LIMIT),
        name="out_ffn",
    )(x1, a, o1, l1, o4, l4, o16, l16, woa, wob, g2, *ffn_w, gf)


def _banded_attention(q, k, v, sink, *, tq, halo, step_rows, emit_lse, name):
    batch, groups, seq_len, wq = q.shape
    wkv = k.shape[-1]
    rows = min(seq_len, step_rows)
    n_seqs = step_rows // rows
    steps = seq_len // rows
    ratio = rows // halo
    halo_blocks = seq_len // halo
    tile = lambda w: pl.BlockSpec((None, n_seqs, rows, w), lambda b, r, i: (b, r, i, 0))
    prev = pl.BlockSpec((None, n_seqs, halo, wkv), lambda b, r, i: (b, r, jnp.maximum(i * ratio - 1, 0), 0))
    nxt = pl.BlockSpec((None, n_seqs, halo, wkv),
                       lambda b, r, i: (b, r, jnp.minimum((i + 1) * ratio, halo_blocks - 1), 0))
    kv_specs = [prev, tile(wkv), nxt]
    in_specs = [tile(wq), *kv_specs, *kv_specs]
    args = [q, k, k, k, v, v, v]
    if sink is not None:
        in_specs = [pl.BlockSpec(memory_space=pltpu.SMEM)] + in_specs
        args = [sink] + args
    out_specs = [tile(wq)]
    out_shape = [jax.ShapeDtypeStruct(q.shape, jnp.bfloat16)]
    if emit_lse:
        out_specs.append(tile(wq))
        out_shape.append(jax.ShapeDtypeStruct(q.shape, jnp.float32))
    tk = tq + 2 * halo
    res = pl.pallas_call(
        functools.partial(_attn_kernel, seq_len=seq_len, tq=tq, halo=halo, sinks=sink is not None,
                          emit_lse=emit_lse),
        grid=(batch, groups // n_seqs, steps),
        in_specs=in_specs,
        out_specs=out_specs,
        out_shape=out_shape,
        scratch_shapes=[pltpu.VMEM((n_seqs, rows + 2 * halo, wkv), jnp.bfloat16),
                        pltpu.VMEM((n_seqs, rows + 2 * halo, wkv), jnp.bfloat16),
                        pltpu.VMEM((3, tq, tk), jnp.float32)],
        compiler_params=pltpu.CompilerParams(dimension_semantics=("arbitrary",) * 3,
                                             vmem_limit_bytes=VMEM_LIMIT),
        name=name,
    )(*args)
    return res


def kernel(x, positions, norm_ffn1, w_gate1, w_up1, w_down1, norm_mix, w_in, a_sink, w_out,
           norm_ffn2, w_gate2, w_up2, w_down2, norm_final):
    batch, seq_len, d = x.shape
    depth = norm_ffn1.shape[0]
    bf = jnp.bfloat16

    inv_freq = 1.0 / (ROPE_THETA ** (jnp.arange(0, HEAD_DIM, 2, dtype=jnp.float32) / HEAD_DIM))
    freq = jnp.tile(inv_freq, LANES // (HEAD_DIM // 2))[None, :]
    pos3d = jnp.repeat(positions.reshape(batch, seq_len // ROPE_GROUPS, ROPE_GROUPS), HEAD_DIM // 2, axis=-1)
    order = jnp.array(A_HEAD_ORDER)
    xc = x

    for l in range(depth):
        ffn1 = (w_gate1[l].astype(bf), w_up1[l].astype(bf), (w_down1[l] * FFN_RES_WEIGHT).astype(bf))
        ffn2 = (w_gate2[l].astype(bf), w_up2[l].astype(bf), (w_down2[l] * FFN_RES_WEIGHT).astype(bf))
        w_in_l = w_in[l]
        w_aq = w_in_l[:, :A_Q_W].reshape(d, A_Q_HEADS, HEAD_DIM)[:, order].reshape(d, A_Q_W)
        w_in_l = jnp.concatenate([w_aq, w_in_l[:, A_Q_W:]], axis=1).astype(bf)
        w_out_l = w_out[l]
        woa = w_out_l[:A_Q_W].reshape(A_Q_HEADS, HEAD_DIM, d)[order].reshape(A_Q_W, d).astype(bf)
        wob = w_out_l[A_Q_W:].astype(bf)

        outs = _ffn_in_proj(xc, pos3d, freq, norm_ffn1[l][None, :], ffn1, norm_mix[l][None, :], w_in_l)
        x1, aq, ak, av, bq, bk, bv = outs[:7]
        n_dil = len(DILATIONS)
        bq_d, bk_d, bv_d = outs[7:7 + n_dil], outs[7 + n_dil:7 + 2 * n_dil], outs[7 + 2 * n_dil:]

        (a_out,) = _banded_attention(aq[:, None], ak[:, None], av[:, None], a_sink[l],
                                     tq=2 * A_HALF_WINDOW, halo=A_HALF_WINDOW, step_rows=ATTN_STEP_ROWS,
                                     emit_lse=False, name="attn_a")
        b_parts = []
        for window, dilation in B_PATTERNS:
            if dilation == 1:
                qkv = (bq[:, None], bk[:, None], bv[:, None])
            else:
                j = DILATIONS.index(dilation)
                qkv = (bq_d[j], bk_d[j], bv_d[j])
            halo = window // (2 * dilation)
            o, lse = _banded_attention(*qkv, None, tq=2 * halo, halo=halo, step_rows=ATTN_STEP_ROWS,
                                       emit_lse=True, name=f"attn_b_d{dilation}")
            if dilation == 1:
                o, lse = o[:, 0], lse[:, 0]
            b_parts.append((o, lse))

        xc = _out_ffn(x1, a_out[:, 0], b_parts, woa, wob, norm_ffn2[l][None, :], ffn2,
                      norm_final[None, :], final_norm=l == depth - 1)
    return xc
```

```python
import functools

import jax
import jax.numpy as jnp
from jax import lax
from jax.experimental import pallas as pl
from jax.experimental.pallas import tpu as pltpu

HEAD_DIM = 64
A_Q_HEADS = 8
A_KV_HEADS = 2
A_HALF_WINDOW = 128
B_HEADS = 8
B_PATTERNS = ((128, 1), (512, 4), (2048, 16))
ROPE_THETA = 10000.0
NORM_EPS = 1e-6
FFN_RES_WEIGHT = 0.5

A_Q_W = A_Q_HEADS * HEAD_DIM
A_KV_W = A_KV_HEADS * HEAD_DIM
B_W = B_HEADS * HEAD_DIM

LANES = 128
NEG_BIG = -1e30
LOG2E = 1.4426950408889634
TOKEN_TILE = 512
FF_CHUNK = 256
ATTN_STEP_ROWS = 2048
VMEM_LIMIT = 56 * 1024 * 1024
A_HEAD_ORDER = (0, 4, 1, 5, 2, 6, 3, 7)
DILATIONS = tuple(d for _, d in B_PATTERNS if d > 1)
ROPE_GROUPS = LANES // (HEAD_DIM // 2)


def _rms_norm(x, g):
    y = x * lax.rsqrt(jnp.mean(x * x, axis=-1, keepdims=True) + NORM_EPS)
    return y * g


def _swiglu_accumulate(h_ref, wg_ref, wu_ref, wd_ref, acc_ref, before_chunk=None):
    d_ff = wg_ref.shape[1]
    for c in range(d_ff // FF_CHUNK):
        if before_chunk and c in before_chunk:
            before_chunk[c]()
        sl = slice(c * FF_CHUNK, (c + 1) * FF_CHUNK)
        g = jnp.dot(h_ref[...], wg_ref[:, sl], preferred_element_type=jnp.float32)
        u = jnp.dot(h_ref[...], wu_ref[:, sl], preferred_element_type=jnp.float32)
        a = (g * jax.nn.sigmoid(g) * u).astype(jnp.bfloat16)
        down = jnp.dot(a, wd_ref[sl, :], preferred_element_type=jnp.float32)
        if c == 0:
            acc_ref[...] = down
        else:
            acc_ref[...] += down


def _rope(t, cos, sin_signed, first_half):
    parts = []
    for j in range(t.shape[1] // LANES):
        tj = t[:, j * LANES:(j + 1) * LANES]
        partner = jnp.where(first_half, pltpu.roll(tj, LANES - HEAD_DIM // 2, 1), pltpu.roll(tj, HEAD_DIM // 2, 1))
        parts.append(tj * cos + partner * sin_signed)
    return parts[0] if len(parts) == 1 else jnp.concatenate(parts, axis=1)


def _write_deinterleaved(val, slab_ref, stage_ref, out_refs):
    tm, width = val.shape
    n_slabs = width // LANES
    d4, d16 = DILATIONS
    ratio = d16 // d4
    rows4 = tm // d4
    for c in range(n_slabs):
        slab_ref[c] = val[:, c * LANES:(c + 1) * LANES]
    for r in range(d4):
        for c in range(n_slabs):
            part = slab_ref[c, pl.ds(r, rows4, stride=d4), :]
            out_refs[d4][r, :, c * LANES:(c + 1) * LANES] = part.astype(out_refs[d4].dtype)
            stage_ref[r * n_slabs + c] = part
    for r in range(d4):
        for q in range(ratio):
            for c in range(n_slabs):
                part = stage_ref[r * n_slabs + c, pl.ds(q, rows4 // ratio, stride=ratio), :]
                out_refs[d16][q * d4 + r, :, c * LANES:(c + 1) * LANES] = part.astype(out_refs[d16].dtype)


def _ffn_in_proj_kernel(x_ref, pos_ref, freq_ref, g1_ref, wg_ref, wu_ref, wd_ref, gm_ref, win_ref,
                        x1_ref, aq_ref, ak_ref, av_ref, bq_ref, bk_ref, bv_ref, *rest):
    n_dil = len(DILATIONS)
    bq_d = dict(zip(DILATIONS, rest[0:n_dil]))
    bk_d = dict(zip(DILATIONS, rest[n_dil:2 * n_dil]))
    bv_d = dict(zip(DILATIONS, rest[2 * n_dil:3 * n_dil]))
    h_ref, h2_cur_ref, h2_next_ref, acc_ref, slab_ref, stage_ref, cos_ref, sin_ref = rest[3 * n_dil:]

    @pl.when(pl.program_id(0) == 0)
    def _():
        h2_next_ref[...] = jnp.zeros_like(h2_next_ref)

    h2_cur_ref[...] = h2_next_ref[...]

    half = HEAD_DIM // 2
    tm = cos_ref.shape[0]
    ang = pos_ref[...].astype(jnp.float32) * freq_ref[...]
    lane_d = lax.broadcasted_iota(jnp.int32, ang.shape, 1)
    first_half_d = (lane_d % HEAD_DIM) < half
    tables = ((jnp.cos(ang), cos_ref, False), (jnp.sin(ang), sin_ref, True))
    for k in range(ROPE_GROUPS):
        in_group = (lane_d // half) == k
        for table, ref, signed in tables:
            one = jnp.where(in_group, table, 0.0)
            full = one + pltpu.roll(one, half, 1)
            full = full + pltpu.roll(full, 2 * half, 1)
            if signed:
                full = jnp.where(first_half_d, -full, full)
            ref[pl.ds(k, tm // ROPE_GROUPS, stride=ROPE_GROUPS), :] = full
    cos = cos_ref[...]
    sin_signed = sin_ref[...]
    lane = lax.broadcasted_iota(jnp.int32, cos.shape, 1)
    first_half = (lane % HEAD_DIM) < half
    scale = HEAD_DIM ** -0.5 * LOG2E

    def proj(src_ref, lo, width):
        return jnp.dot(src_ref[...], win_ref[:, lo:lo + width], preferred_element_type=jnp.float32)

    offs = {}
    o = 0
    for name, width in (("aq", A_Q_W), ("ak", A_KV_W), ("av", A_KV_W), ("bq", B_W), ("bk", B_W), ("bv", B_W)):
        offs[name] = (o, width)
        o += width
    rope = lambda t: _rope(t, cos, sin_signed, first_half)

    def emit_bv():
        bv = proj(h2_next_ref, *offs["bv"])
        bv_ref[...] = bv.astype(bv_ref.dtype)
        _write_deinterleaved(bv, slab_ref, stage_ref, bv_d)

    def emit_bk():
        bk = rope(proj(h2_next_ref, *offs["bk"]))
        bk_ref[...] = bk.astype(bk_ref.dtype)
        _write_deinterleaved(bk, slab_ref, stage_ref, bk_d)

    def emit_bq():
        bq = rope(proj(h2_next_ref, *offs["bq"])) * scale
        bq_ref[...] = bq.astype(bq_ref.dtype)
        _write_deinterleaved(bq, slab_ref, stage_ref, bq_d)

    x = x_ref[...]
    h_ref[...] = _rms_norm(x, g1_ref[...]).astype(jnp.bfloat16)
    _swiglu_accumulate(h_ref, wg_ref, wu_ref, wd_ref, acc_ref, before_chunk={0: emit_bv, 3: emit_bk, 6: emit_bq})
    x1 = x + acc_ref[...]
    x1_ref[...] = x1
    h2_next_ref[...] = _rms_norm(x1, gm_ref[...]).astype(jnp.bfloat16)

    aq_ref[...] = (rope(proj(h2_cur_ref, *offs["aq"])) * scale).astype(aq_ref.dtype)
    ak_ref[...] = rope(proj(h2_cur_ref, *offs["ak"])).astype(ak_ref.dtype)
    av_ref[...] = proj(h2_cur_ref, *offs["av"]).astype(av_ref.dtype)


def _interleave_into(slab_ref, blk_ref, d, stage_ref=None):
    n_slabs = slab_ref.shape[0]
    rows = blk_ref.shape[1]
    if stage_ref is None:
        for r in range(d):
            for c in range(n_slabs):
                slab_ref[c, pl.ds(r, rows, stride=d), :] = blk_ref[r, :, c * LANES:(c + 1) * LANES].astype(jnp.float32)
        return
    d4 = DILATIONS[0]
    ratio = d // d4
    for r in range(d4):
        for c in range(n_slabs):
            for q in range(ratio):
                part = blk_ref[q * d4 + r, :, c * LANES:(c + 1) * LANES].astype(jnp.float32)
                stage_ref[r * n_slabs + c, pl.ds(q, rows, stride=ratio), :] = part
            slab_ref[c, pl.ds(r, rows * ratio, stride=d4), :] = stage_ref[r * n_slabs + c]


def _out_ffn_kernel(x1_ref, a_ref, o1_ref, l1_ref, o4_ref, l4_ref, o16_ref, l16_ref,
                    wo_ref, g2_ref, wg_ref, wu_ref, wd_ref, gf_ref,
                    out_ref, h_ref, acc_ref, so4_ref, sl4_ref, so16_ref, sl16_ref, stage_o_ref, stage_l_ref,
                    *, final_norm):
    d4, d16 = DILATIONS
    _interleave_into(so4_ref, o4_ref, d4)
    _interleave_into(sl4_ref, l4_ref, d4)
    _interleave_into(so16_ref, o16_ref, d16, stage_o_ref)
    _interleave_into(sl16_ref, l16_ref, d16, stage_l_ref)

    mix = jnp.dot(a_ref[...], wo_ref[:A_Q_W, :], preferred_element_type=jnp.float32)
    for c in range(B_W // LANES):
        sl = slice(c * LANES, (c + 1) * LANES)
        la, lb, lc = l1_ref[:, sl], sl4_ref[c], sl16_ref[c]
        m = jnp.maximum(jnp.maximum(la, lb), lc)
        wa, wb, wc = jnp.exp2(la - m), jnp.exp2(lb - m), jnp.exp2(lc - m)
        num = wa * o1_ref[:, sl].astype(jnp.float32) + wb * so4_ref[c] + wc * so16_ref[c]
        b = (num / (wa + wb + wc)).astype(jnp.bfloat16)
        mix += jnp.dot(b, wo_ref[A_Q_W + c * LANES:A_Q_W + (c + 1) * LANES, :], preferred_element_type=jnp.float32)
    x2 = x1_ref[...] + mix
    h_ref[...] = _rms_norm(x2, g2_ref[...]).astype(jnp.bfloat16)
    _swiglu_accumulate(h_ref, wg_ref, wu_ref, wd_ref, acc_ref)
    x3 = x2 + acc_ref[...]
    out_ref[...] = _rms_norm(x3, gf_ref[...]) if final_norm else x3


def _attn_kernel(*refs, seq_len, tq, halo, sinks, emit_lse, cast_scales):
    if sinks:
        sink_ref, refs = refs[0], refs[1:]
    q_ref, kp_ref, kc_ref, kn_ref, vp_ref, vc_ref, vn_ref = refs[:7]
    refs = refs[7:]
    cast_in, refs = refs[:len(cast_scales)], refs[len(cast_scales):]
    o_ref, refs = refs[0], refs[1:]
    if emit_lse:
        lse_ref, refs = refs[0], refs[1:]
    cast_out, refs = refs[:len(cast_scales)], refs[len(cast_scales):]
    kbuf, vbuf, bias_ref = refs

    for src, dst, scale in zip(cast_in, cast_out, cast_scales):
        dst[...] = (src[...] if scale == 1 else src[...] * scale).astype(dst.dtype)

    n_seqs, rows = q_ref.shape[:2]
    tk = tq + 2 * halo
    tiles = rows // tq
    tiles_in_seq = seq_len // tq
    kv_tiles = kbuf.shape[2] // LANES
    q_tiles = q_ref.shape[2] // LANES
    step = pl.program_id(2)

    @pl.when((pl.program_id(0) == 0) & (pl.program_id(1) == 0) & (step == 0))
    def _():
        row = lax.broadcasted_iota(jnp.int32, (tq, tk), 0)
        col = lax.broadcasted_iota(jnp.int32, (tq, tk), 1)
        band = jnp.abs(row + halo - col) <= halo
        bias_ref[0] = jnp.where(band & (col >= halo), 0.0, NEG_BIG)
        bias_ref[1] = jnp.where(band, 0.0, NEG_BIG)
        bias_ref[2] = jnp.where(band & (col < tq + halo), 0.0, NEG_BIG)

    kbuf[:, 0:halo, :] = kp_ref[...]
    kbuf[:, halo:halo + rows, :] = kc_ref[...]
    kbuf[:, halo + rows:, :] = kn_ref[...]
    vbuf[:, 0:halo, :] = vp_ref[...]
    vbuf[:, halo:halo + rows, :] = vc_ref[...]
    vbuf[:, halo + rows:, :] = vn_ref[...]

    lane = lax.broadcasted_iota(jnp.int32, (tq, LANES), 1)
    low = lane < HEAD_DIM
    ones = jnp.ones((tk, LANES), jnp.bfloat16)
    zero_q = jnp.zeros((tq, LANES), jnp.bfloat16)

    def scores(n, t, c):
        kc = c if kv_tiles == q_tiles else 0
        r0 = t * tq
        qp = q_ref[n, r0:r0 + tq, c * LANES:(c + 1) * LANES]
        q2 = jnp.concatenate([jnp.where(low, qp, zero_q), jnp.where(low, zero_q, qp)], axis=0)
        kp = kbuf[n, r0:r0 + tk, kc * LANES:(kc + 1) * LANES]
        return lax.dot_general(q2, kp, (((1,), (1,)), ((), ())), preferred_element_type=jnp.float32)

    def finish(n, t, c, s):
        kc = c if kv_tiles == q_tiles else 0
        r0 = t * tq
        g = step * tiles + t
        bias = bias_ref[jnp.where(g == 0, 0, jnp.where(g == tiles_in_seq - 1, 2, 1))]
        vext = jnp.concatenate([vbuf[n, r0:r0 + tk, kc * LANES:(kc + 1) * LANES], ones], axis=1)
        s_lo = s[:tq] + bias
        s_hi = s[tq:] + bias
        m_lo = jnp.max(s_lo, axis=-1, keepdims=True)
        m_hi = jnp.max(s_hi, axis=-1, keepdims=True)
        if sinks:
            sink_lo = sink_ref[A_HEAD_ORDER[2 * c]] * LOG2E
            sink_hi = sink_ref[A_HEAD_ORDER[2 * c + 1]] * LOG2E
            m_lo = jnp.maximum(m_lo, sink_lo)
            m_hi = jnp.maximum(m_hi, sink_hi)
        p = jnp.concatenate([jnp.exp2(s_lo - m_lo), jnp.exp2(s_hi - m_hi)], axis=0).astype(jnp.bfloat16)
        o2 = jnp.dot(p, vext, preferred_element_type=jnp.float32)
        num = jnp.where(low, o2[:tq, :LANES], o2[tq:, :LANES])
        den = jnp.where(low, o2[:tq, LANES:], o2[tq:, LANES:])
        m = jnp.where(low, m_lo, m_hi)
        if sinks:
            den = den + jnp.exp2(jnp.where(low, sink_lo, sink_hi) - m)
        o_ref[n, r0:r0 + tq, c * LANES:(c + 1) * LANES] = (num / den).astype(o_ref.dtype)
        if emit_lse:
            lse_ref[n, r0:r0 + tq, c * LANES:(c + 1) * LANES] = m + jnp.log(den) * LOG2E

    for n, t, c in [(n, t, c) for n in range(n_seqs) for t in range(tiles) for c in range(q_tiles)]:
        finish(n, t, c, scores(n, t, c))


def _const_spec(shape):
    zeros = (0,) * len(shape)
    return pl.BlockSpec(shape, lambda *_: zeros, pipeline_mode=pl.Buffered(1))


def _ffn_in_proj(x3d, pos3d, freq, g1, ffn_w, gm, w_in):
    batch, seq_len, d = x3d.shape
    tm = TOKEN_TILE
    tiles = seq_len // tm
    bf = jnp.bfloat16
    n_tiles = batch * tiles
    ffn_tile = lambda t: jnp.minimum(t, n_tiles - 1)
    proj_tile = lambda t: jnp.maximum(t - 1, 0)
    row = lambda w, which: pl.BlockSpec((None, tm, w), lambda t: (which(t) // tiles, which(t) % tiles, 0))
    dil = lambda dd: pl.BlockSpec((None, dd, tm // dd, B_W),
                                  lambda t: (proj_tile(t) // tiles, 0, proj_tile(t) % tiles, 0))
    nat = lambda w, dt: jax.ShapeDtypeStruct((batch, seq_len, w), dt)
    dil_shape = lambda dd: jax.ShapeDtypeStruct((batch, dd, seq_len // dd, B_W), bf)
    out_specs = [row(d, ffn_tile)] + [row(w, proj_tile) for w in (A_Q_W, A_KV_W, A_KV_W, B_W, B_W, B_W)]
    out_shape = [nat(d, jnp.float32), nat(A_Q_W, bf), nat(A_KV_W, bf), nat(A_KV_W, bf),
                 nat(B_W, bf), nat(B_W, bf), nat(B_W, bf)]
    for _ in range(3):
        out_specs += [dil(dd) for dd in DILATIONS]
        out_shape += [dil_shape(dd) for dd in DILATIONS]
    return pl.pallas_call(
        _ffn_in_proj_kernel,
        grid=(n_tiles + 1,),
        in_specs=[row(d, ffn_tile),
                  pl.BlockSpec((None, tm // ROPE_GROUPS, LANES),
                               lambda t: (proj_tile(t) // tiles, proj_tile(t) % tiles, 0)),
                  _const_spec(freq.shape), _const_spec(g1.shape),
                  *[_const_spec(w.shape) for w in ffn_w], _const_spec(gm.shape), _const_spec(w_in.shape)],
        out_specs=out_specs,
        out_shape=out_shape,
        scratch_shapes=[pltpu.VMEM((tm, d), jnp.bfloat16), pltpu.VMEM((tm, d), jnp.bfloat16),
                        pltpu.VMEM((tm, d), jnp.bfloat16), pltpu.VMEM((tm, d), jnp.float32),
                        pltpu.VMEM((B_W // LANES, tm, LANES), jnp.float32),
                        pltpu.VMEM((DILATIONS[0] * B_W // LANES, tm // DILATIONS[0], LANES), jnp.float32),
                        pltpu.VMEM((tm, LANES), jnp.float32), pltpu.VMEM((tm, LANES), jnp.float32)],
        compiler_params=pltpu.CompilerParams(dimension_semantics=("arbitrary",),
                                             vmem_limit_bytes=VMEM_LIMIT),
        name="ffn_in_proj",
    )(x3d, pos3d, freq, g1, *ffn_w, gm, w_in)


def _out_ffn(x1, a, b_parts, wo, g2, ffn_w, gf, final_norm):
    batch, seq_len, d = x1.shape
    tm = TOKEN_TILE
    tiles = seq_len // tm
    row = lambda w: pl.BlockSpec((None, tm, w), lambda b, i: (b, i, 0))
    dil = lambda dd: pl.BlockSpec((None, dd, tm // dd, B_W), lambda b, i: (b, 0, i, 0))
    (o1, l1), (o4, l4), (o16, l16) = b_parts
    d4, d16 = DILATIONS
    slab = lambda: pltpu.VMEM((B_W // LANES, tm, LANES), jnp.float32)
    stage = lambda: pltpu.VMEM((d4 * B_W // LANES, tm // d4, LANES), jnp.float32)
    return pl.pallas_call(
        functools.partial(_out_ffn_kernel, final_norm=final_norm),
        grid=(batch, tiles),
        in_specs=[row(d), row(A_Q_W), row(B_W), row(B_W), dil(d4), dil(d4), dil(d16), dil(d16),
                  _const_spec(wo.shape), _const_spec(g2.shape),
                  *[_const_spec(w.shape) for w in ffn_w], _const_spec(gf.shape)],
        out_specs=row(d),
        out_shape=jax.ShapeDtypeStruct((batch, seq_len, d), jnp.float32),
        scratch_shapes=[pltpu.VMEM((tm, d), jnp.bfloat16), pltpu.VMEM((tm, d), jnp.float32),
                        slab(), slab(), slab(), slab(), stage(), stage()],
        compiler_params=pltpu.CompilerParams(dimension_semantics=("arbitrary", "arbitrary"),
                                             vmem_limit_bytes=VMEM_LIMIT),
        name="out_ffn",
    )(x1, a, o1, l1, o4, l4, o16, l16, wo, g2, *ffn_w, gf)


def _banded_attention(q, k, v, sink, *, tq, halo, step_rows, emit_lse, name, casts=()):
    batch, groups, seq_len, wq = q.shape
    wkv = k.shape[-1]
    rows = min(seq_len, step_rows)
    n_seqs = step_rows // rows
    steps = seq_len // rows
    ratio = rows // halo
    halo_blocks = seq_len // halo
    tile = lambda w: pl.BlockSpec((None, n_seqs, rows, w), lambda b, r, i: (b, r, i, 0))
    prev = pl.BlockSpec((None, n_seqs, halo, wkv), lambda b, r, i: (b, r, jnp.maximum(i * ratio - 1, 0), 0))
    nxt = pl.BlockSpec((None, n_seqs, halo, wkv),
                       lambda b, r, i: (b, r, jnp.minimum((i + 1) * ratio, halo_blocks - 1), 0))
    kv_specs = [prev, tile(wkv), nxt]
    in_specs = [tile(wq), *kv_specs, *kv_specs]
    args = [q, k, k, k, v, v, v]
    if sink is not None:
        in_specs = [pl.BlockSpec(memory_space=pltpu.SMEM)] + in_specs
        args = [sink] + args
    out_specs = [tile(wq)]
    out_shape = [jax.ShapeDtypeStruct(q.shape, jnp.bfloat16)]
    if emit_lse:
        out_specs.append(tile(wq))
        out_shape.append(jax.ShapeDtypeStruct(q.shape, jnp.float32))
    grid = (batch, groups // n_seqs, steps)
    n_steps = grid[0] * grid[1] * grid[2]
    for w, _ in casts:
        spec = pl.BlockSpec((w.shape[0] // n_steps, w.shape[1]),
                            lambda b, r, i: ((b * grid[1] + r) * grid[2] + i, 0))
        in_specs.append(spec)
        args.append(w)
        out_specs.append(spec)
        out_shape.append(jax.ShapeDtypeStruct(w.shape, jnp.bfloat16))
    tk = tq + 2 * halo
    res = pl.pallas_call(
        functools.partial(_attn_kernel, seq_len=seq_len, tq=tq, halo=halo, sinks=sink is not None,
                          emit_lse=emit_lse, cast_scales=tuple(scale for _, scale in casts)),
        grid=grid,
        in_specs=in_specs,
        out_specs=out_specs,
        out_shape=out_shape,
        scratch_shapes=[pltpu.VMEM((n_seqs, rows + 2 * halo, wkv), jnp.bfloat16),
                        pltpu.VMEM((n_seqs, rows + 2 * halo, wkv), jnp.bfloat16),
                        pltpu.VMEM((3, tq, tk), jnp.float32)],
        compiler_params=pltpu.CompilerParams(dimension_semantics=("arbitrary",) * 3,
                                             vmem_limit_bytes=VMEM_LIMIT),
        name=name,
    )(*args)
    return res


def kernel(x, positions, norm_ffn1, w_gate1, w_up1, w_down1, norm_mix, w_in, a_sink, w_out,
           norm_ffn2, w_gate2, w_up2, w_down2, norm_final):
    batch, seq_len, d = x.shape
    depth = norm_ffn1.shape[0]
    bf = jnp.bfloat16

    inv_freq = 1.0 / (ROPE_THETA ** (jnp.arange(0, HEAD_DIM, 2, dtype=jnp.float32) / HEAD_DIM))
    freq = jnp.tile(inv_freq, LANES // (HEAD_DIM // 2))[None, :]
    pos3d = jnp.repeat(positions.reshape(batch, seq_len // ROPE_GROUPS, ROPE_GROUPS), HEAD_DIM // 2, axis=-1)
    order = jnp.array(A_HEAD_ORDER)
    xc = x

    for l in range(depth):
        ffn1 = (w_gate1[l].astype(bf), w_up1[l].astype(bf), (w_down1[l] * FFN_RES_WEIGHT).astype(bf))
        w_in_l = w_in[l]
        w_aq = w_in_l[:, :A_Q_W].reshape(d, A_Q_HEADS, HEAD_DIM)[:, order].reshape(d, A_Q_W)
        w_in_l = jnp.concatenate([w_aq, w_in_l[:, A_Q_W:]], axis=1).astype(bf)
        w_out_l = w_out[l]
        woa = w_out_l[:A_Q_W].reshape(A_Q_HEADS, HEAD_DIM, d)[order].reshape(A_Q_W, d)
        w_out_l = jnp.concatenate([woa, w_out_l[A_Q_W:]], axis=0)

        outs = _ffn_in_proj(xc, pos3d, freq, norm_ffn1[l][None, :], ffn1, norm_mix[l][None, :], w_in_l)
        x1, aq, ak, av, bq, bk, bv = outs[:7]
        n_dil = len(DILATIONS)
        bq_d, bk_d, bv_d = outs[7:7 + n_dil], outs[7 + n_dil:7 + 2 * n_dil], outs[7 + 2 * n_dil:]

        a_out, *ffn2, wo = _banded_attention(
            aq[:, None], ak[:, None], av[:, None], a_sink[l],
            tq=2 * A_HALF_WINDOW, halo=A_HALF_WINDOW, step_rows=ATTN_STEP_ROWS, emit_lse=False, name="attn_a",
            casts=((w_gate2[l], 1), (w_up2[l], 1), (w_down2[l], FFN_RES_WEIGHT), (w_out_l, 1)))
        b_parts = []
        for window, dilation in B_PATTERNS:
            if dilation == 1:
                qkv = (bq[:, None], bk[:, None], bv[:, None])
            else:
                j = DILATIONS.index(dilation)
                qkv = (bq_d[j], bk_d[j], bv_d[j])
            halo = window // (2 * dilation)
            o, lse = _banded_attention(*qkv, None, tq=2 * halo, halo=halo, step_rows=ATTN_STEP_ROWS,
                                       emit_lse=True, name=f"attn_b_d{dilation}")
            if dilation == 1:
                o, lse = o[:, 0], lse[:, 0]
            b_parts.append((o, lse))

        xc = _out_ffn(x1, a_out[:, 0], b_parts, wo, norm_ffn2[l][None, :], ffn2,
                      norm_final[None, :], final_norm=l == depth - 1)
    return xc
```

```python
import functools

import jax
import jax.numpy as jnp
from jax import lax
from jax.experimental import pallas as pl
from jax.experimental.pallas import tpu as pltpu

HEAD_DIM = 64
A_Q_HEADS = 8
A_KV_HEADS = 2
A_HALF_WINDOW = 128
B_HEADS = 8
B_PATTERNS = ((128, 1), (512, 4), (2048, 16))
ROPE_THETA = 10000.0
NORM_EPS = 1e-6
FFN_RES_WEIGHT = 0.5

A_Q_W = A_Q_HEADS * HEAD_DIM
A_KV_W = A_KV_HEADS * HEAD_DIM
B_W = B_HEADS * HEAD_DIM

LANES = 128
NEG_BIG = -1e30
LOG2E = 1.4426950408889634
TOKEN_TILE = 512
FF_CHUNK = 256
ATTN_STEP_ROWS = 2048
VMEM_LIMIT = 56 * 1024 * 1024
A_HEAD_ORDER = (0, 4, 1, 5, 2, 6, 3, 7)
DILATIONS = tuple(d for _, d in B_PATTERNS if d > 1)
ROPE_GROUPS = LANES // (HEAD_DIM // 2)


def _rms_norm(x, g):
    y = x * lax.rsqrt(jnp.mean(x * x, axis=-1, keepdims=True) + NORM_EPS)
    return y * g


def _swiglu_accumulate(h_ref, wg_ref, wu_ref, wd_ref, acc_ref, before_chunk=None):
    d_ff = wg_ref.shape[1]
    for c in range(d_ff // FF_CHUNK):
        if before_chunk and c in before_chunk:
            before_chunk[c]()
        sl = slice(c * FF_CHUNK, (c + 1) * FF_CHUNK)
        g = jnp.dot(h_ref[...], wg_ref[:, sl], preferred_element_type=jnp.float32)
        u = jnp.dot(h_ref[...], wu_ref[:, sl], preferred_element_type=jnp.float32)
        a = (g * jax.nn.sigmoid(g) * u).astype(jnp.bfloat16)
        down = jnp.dot(a, wd_ref[sl, :], preferred_element_type=jnp.float32)
        if c == 0:
            acc_ref[...] = down
        else:
            acc_ref[...] += down


def _rope(t, cos, sin_signed, first_half):
    parts = []
    for j in range(t.shape[1] // LANES):
        tj = t[:, j * LANES:(j + 1) * LANES]
        partner = jnp.where(first_half, pltpu.roll(tj, LANES - HEAD_DIM // 2, 1), pltpu.roll(tj, HEAD_DIM // 2, 1))
        parts.append(tj * cos + partner * sin_signed)
    return parts[0] if len(parts) == 1 else jnp.concatenate(parts, axis=1)


def _write_deinterleaved(val, slab_ref, stage_ref, out_refs):
    tm, width = val.shape
    n_slabs = width // LANES
    d4, d16 = DILATIONS
    ratio = d16 // d4
    rows4 = tm // d4
    for c in range(n_slabs):
        slab_ref[c] = val[:, c * LANES:(c + 1) * LANES]
    for r in range(d4):
        for c in range(n_slabs):
            part = slab_ref[c, pl.ds(r, rows4, stride=d4), :]
            out_refs[d4][r, :, c * LANES:(c + 1) * LANES] = part.astype(out_refs[d4].dtype)
            stage_ref[r * n_slabs + c] = part
    for r in range(d4):
        for q in range(ratio):
            for c in range(n_slabs):
                part = stage_ref[r * n_slabs + c, pl.ds(q, rows4 // ratio, stride=ratio), :]
                out_refs[d16][q * d4 + r, :, c * LANES:(c + 1) * LANES] = part.astype(out_refs[d16].dtype)


def _ffn_in_proj_kernel(x_ref, pos_ref, freq_ref, g1_ref, wg_ref, wu_ref, wd_ref, gm_ref, win_ref,
                        x1_ref, aq_ref, ak_ref, av_ref, bq_ref, bk_ref, bv_ref, *rest):
    n_dil = len(DILATIONS)
    bq_d = dict(zip(DILATIONS, rest[0:n_dil]))
    bk_d = dict(zip(DILATIONS, rest[n_dil:2 * n_dil]))
    bv_d = dict(zip(DILATIONS, rest[2 * n_dil:3 * n_dil]))
    h_ref, h2_cur_ref, h2_next_ref, acc_ref, slab_ref, stage_ref, cos_ref, sin_ref, waq_ref = rest[3 * n_dil:]

    @pl.when(pl.program_id(0) == 0)
    def _():
        h2_next_ref[...] = jnp.zeros_like(h2_next_ref)
        in_low = lax.broadcasted_iota(jnp.int32, (win_ref.shape[0], LANES), 1) < HEAD_DIM
        per_group = A_Q_HEADS // A_KV_HEADS
        for c in range(A_Q_W // LANES):
            halves = []
            for dst_half, head in enumerate((c, c + per_group)):
                lo = (head // 2) * LANES
                src = win_ref[:, lo:lo + LANES].astype(jnp.float32)
                halves.append(src if head % 2 == dst_half else pltpu.roll(src, HEAD_DIM, 1))
            waq_ref[:, c * LANES:(c + 1) * LANES] = jnp.where(in_low, halves[0], halves[1]).astype(waq_ref.dtype)

    h2_cur_ref[...] = h2_next_ref[...]

    half = HEAD_DIM // 2
    tm = cos_ref.shape[0]
    ang = pos_ref[...].astype(jnp.float32) * freq_ref[...]
    lane_d = lax.broadcasted_iota(jnp.int32, ang.shape, 1)
    first_half_d = (lane_d % HEAD_DIM) < half
    tables = ((jnp.cos(ang), cos_ref, False), (jnp.sin(ang), sin_ref, True))
    for k in range(ROPE_GROUPS):
        in_group = (lane_d // half) == k
        for table, ref, signed in tables:
            one = jnp.where(in_group, table, 0.0)
            full = one + pltpu.roll(one, half, 1)
            full = full + pltpu.roll(full, 2 * half, 1)
            if signed:
                full = jnp.where(first_half_d, -full, full)
            ref[pl.ds(k, tm // ROPE_GROUPS, stride=ROPE_GROUPS), :] = full
    cos = cos_ref[...]
    sin_signed = sin_ref[...]
    lane = lax.broadcasted_iota(jnp.int32, cos.shape, 1)
    first_half = (lane % HEAD_DIM) < half
    scale = HEAD_DIM ** -0.5 * LOG2E

    def proj(src_ref, lo, width):
        return jnp.dot(src_ref[...], win_ref[:, lo:lo + width], preferred_element_type=jnp.float32)

    offs = {}
    o = 0
    for name, width in (("aq", A_Q_W), ("ak", A_KV_W), ("av", A_KV_W), ("bq", B_W), ("bk", B_W), ("bv", B_W)):
        offs[name] = (o, width)
        o += width
    rope = lambda t: _rope(t, cos, sin_signed, first_half)

    def emit_bv():
        bv = proj(h2_next_ref, *offs["bv"])
        bv_ref[...] = bv.astype(bv_ref.dtype)
        _write_deinterleaved(bv, slab_ref, stage_ref, bv_d)

    def emit_bk():
        bk = rope(proj(h2_next_ref, *offs["bk"]))
        bk_ref[...] = bk.astype(bk_ref.dtype)
        _write_deinterleaved(bk, slab_ref, stage_ref, bk_d)

    def emit_bq():
        bq = rope(proj(h2_next_ref, *offs["bq"])) * scale
        bq_ref[...] = bq.astype(bq_ref.dtype)
        _write_deinterleaved(bq, slab_ref, stage_ref, bq_d)

    x = x_ref[...]
    h_ref[...] = _rms_norm(x, g1_ref[...]).astype(jnp.bfloat16)
    _swiglu_accumulate(h_ref, wg_ref, wu_ref, wd_ref, acc_ref, before_chunk={0: emit_bv, 3: emit_bk, 6: emit_bq})
    x1 = x + acc_ref[...]
    x1_ref[...] = x1
    h2_next_ref[...] = _rms_norm(x1, gm_ref[...]).astype(jnp.bfloat16)

    aq = jnp.dot(h2_cur_ref[...], waq_ref[...], preferred_element_type=jnp.float32)
    aq_ref[...] = (rope(aq) * scale).astype(aq_ref.dtype)
    ak_ref[...] = rope(proj(h2_cur_ref, *offs["ak"])).astype(ak_ref.dtype)
    av_ref[...] = proj(h2_cur_ref, *offs["av"]).astype(av_ref.dtype)


def _interleave_into(slab_ref, blk_ref, d, stage_ref=None):
    n_slabs = slab_ref.shape[0]
    rows = blk_ref.shape[1]
    if stage_ref is None:
        for r in range(d):
            for c in range(n_slabs):
                slab_ref[c, pl.ds(r, rows, stride=d), :] = blk_ref[r, :, c * LANES:(c + 1) * LANES].astype(jnp.float32)
        return
    d4 = DILATIONS[0]
    ratio = d // d4
    for r in range(d4):
        for c in range(n_slabs):
            for q in range(ratio):
                part = blk_ref[q * d4 + r, :, c * LANES:(c + 1) * LANES].astype(jnp.float32)
                stage_ref[r * n_slabs + c, pl.ds(q, rows, stride=ratio), :] = part
            slab_ref[c, pl.ds(r, rows * ratio, stride=d4), :] = stage_ref[r * n_slabs + c]


def _out_ffn_kernel(x1_ref, a_ref, o1_ref, l1_ref, o4_ref, l4_ref, o16_ref, l16_ref,
                    wo_ref, g2_ref, wg_ref, wu_ref, wd_ref, gf_ref,
                    out_ref, h_ref, acc_ref, so4_ref, sl4_ref, so16_ref, sl16_ref, stage_o_ref, stage_l_ref,
                    *, final_norm):
    d4, d16 = DILATIONS
    _interleave_into(so4_ref, o4_ref, d4)
    _interleave_into(sl4_ref, l4_ref, d4)
    _interleave_into(so16_ref, o16_ref, d16, stage_o_ref)
    _interleave_into(sl16_ref, l16_ref, d16, stage_l_ref)

    mix = jnp.dot(a_ref[...], wo_ref[:A_Q_W, :], preferred_element_type=jnp.float32)
    for c in range(B_W // LANES):
        sl = slice(c * LANES, (c + 1) * LANES)
        la, lb, lc = l1_ref[:, sl], sl4_ref[c], sl16_ref[c]
        m = jnp.maximum(jnp.maximum(la, lb), lc)
        wa, wb, wc = jnp.exp2(la - m), jnp.exp2(lb - m), jnp.exp2(lc - m)
        num = wa * o1_ref[:, sl].astype(jnp.float32) + wb * so4_ref[c] + wc * so16_ref[c]
        b = (num / (wa + wb + wc)).astype(jnp.bfloat16)
        mix += jnp.dot(b, wo_ref[A_Q_W + c * LANES:A_Q_W + (c + 1) * LANES, :], preferred_element_type=jnp.float32)
    x2 = x1_ref[...] + mix
    h_ref[...] = _rms_norm(x2, g2_ref[...]).astype(jnp.bfloat16)
    _swiglu_accumulate(h_ref, wg_ref, wu_ref, wd_ref, acc_ref)
    x3 = x2 + acc_ref[...]
    out_ref[...] = _rms_norm(x3, gf_ref[...]) if final_norm else x3


def _attn_kernel(*refs, seq_len, tq, halo, sinks, emit_lse, cast_scales):
    if sinks:
        sink_ref, refs = refs[0], refs[1:]
    q_ref, kp_ref, kc_ref, kn_ref, vp_ref, vc_ref, vn_ref = refs[:7]
    refs = refs[7:]
    cast_in, refs = refs[:len(cast_scales)], refs[len(cast_scales):]
    o_ref, refs = refs[0], refs[1:]
    if emit_lse:
        lse_ref, refs = refs[0], refs[1:]
    cast_out, refs = refs[:len(cast_scales)], refs[len(cast_scales):]
    kbuf, vbuf, bias_ref = refs

    for src, dst, scale in zip(cast_in, cast_out, cast_scales):
        dst[...] = (src[...] if scale == 1 else src[...] * scale).astype(dst.dtype)

    n_seqs, rows = q_ref.shape[:2]
    tk = tq + 2 * halo
    tiles = rows // tq
    tiles_in_seq = seq_len // tq
    kv_tiles = kbuf.shape[2] // LANES
    q_tiles = q_ref.shape[2] // LANES
    step = pl.program_id(2)

    @pl.when((pl.program_id(0) == 0) & (pl.program_id(1) == 0) & (step == 0))
    def _():
        row = lax.broadcasted_iota(jnp.int32, (tq, tk), 0)
        col = lax.broadcasted_iota(jnp.int32, (tq, tk), 1)
        band = jnp.abs(row + halo - col) <= halo
        bias_ref[0] = jnp.where(band & (col >= halo), 0.0, NEG_BIG)
        bias_ref[1] = jnp.where(band, 0.0, NEG_BIG)
        bias_ref[2] = jnp.where(band & (col < tq + halo), 0.0, NEG_BIG)

    kbuf[:, 0:halo, :] = kp_ref[...]
    kbuf[:, halo:halo + rows, :] = kc_ref[...]
    kbuf[:, halo + rows:, :] = kn_ref[...]
    vbuf[:, 0:halo, :] = vp_ref[...]
    vbuf[:, halo:halo + rows, :] = vc_ref[...]
    vbuf[:, halo + rows:, :] = vn_ref[...]

    lane = lax.broadcasted_iota(jnp.int32, (tq, LANES), 1)
    low = lane < HEAD_DIM
    ones = jnp.ones((tk, LANES), jnp.bfloat16)
    zero_q = jnp.zeros((tq, LANES), jnp.bfloat16)

    def scores(n, t, c):
        kc = c if kv_tiles == q_tiles else 0
        r0 = t * tq
        qp = q_ref[n, r0:r0 + tq, c * LANES:(c + 1) * LANES]
        q2 = jnp.concatenate([jnp.where(low, qp, zero_q), jnp.where(low, zero_q, qp)], axis=0)
        kp = kbuf[n, r0:r0 + tk, kc * LANES:(kc + 1) * LANES]
        return lax.dot_general(q2, kp, (((1,), (1,)), ((), ())), preferred_element_type=jnp.float32)

    def finish(n, t, c, s):
        kc = c if kv_tiles == q_tiles else 0
        r0 = t * tq
        g = step * tiles + t
        bias = bias_ref[jnp.where(g == 0, 0, jnp.where(g == tiles_in_seq - 1, 2, 1))]
        vext = jnp.concatenate([vbuf[n, r0:r0 + tk, kc * LANES:(kc + 1) * LANES], ones], axis=1)
        s_lo = s[:tq] + bias
        s_hi = s[tq:] + bias
        m_lo = jnp.max(s_lo, axis=-1, keepdims=True)
        m_hi = jnp.max(s_hi, axis=-1, keepdims=True)
        if sinks:
            sink_lo = sink_ref[A_HEAD_ORDER[2 * c]] * LOG2E
            sink_hi = sink_ref[A_HEAD_ORDER[2 * c + 1]] * LOG2E
            m_lo = jnp.maximum(m_lo, sink_lo)
            m_hi = jnp.maximum(m_hi, sink_hi)
        p = jnp.concatenate([jnp.exp2(s_lo - m_lo), jnp.exp2(s_hi - m_hi)], axis=0).astype(jnp.bfloat16)
        o2 = jnp.dot(p, vext, preferred_element_type=jnp.float32)
        num = jnp.where(low, o2[:tq, :LANES], o2[tq:, :LANES])
        den = jnp.where(low, o2[:tq, LANES:], o2[tq:, LANES:])
        m = jnp.where(low, m_lo, m_hi)
        if sinks:
            den = den + jnp.exp2(jnp.where(low, sink_lo, sink_hi) - m)
        o_ref[n, r0:r0 + tq, c * LANES:(c + 1) * LANES] = (num / den).astype(o_ref.dtype)
        if emit_lse:
            lse_ref[n, r0:r0 + tq, c * LANES:(c + 1) * LANES] = m + jnp.log(den) * LOG2E

    for n, t, c in [(n, t, c) for n in range(n_seqs) for t in range(tiles) for c in range(q_tiles)]:
        finish(n, t, c, scores(n, t, c))


def _const_spec(shape):
    zeros = (0,) * len(shape)
    return pl.BlockSpec(shape, lambda *_: zeros, pipeline_mode=pl.Buffered(1))


def _ffn_in_proj(x3d, pos3d, freq, g1, ffn_w, gm, w_in):
    batch, seq_len, d = x3d.shape
    tm = TOKEN_TILE
    tiles = seq_len // tm
    bf = jnp.bfloat16
    n_tiles = batch * tiles
    ffn_tile = lambda t: jnp.minimum(t, n_tiles - 1)
    proj_tile = lambda t: jnp.maximum(t - 1, 0)
    row = lambda w, which: pl.BlockSpec((None, tm, w), lambda t: (which(t) // tiles, which(t) % tiles, 0))
    dil = lambda dd: pl.BlockSpec((None, dd, tm // dd, B_W),
                                  lambda t: (proj_tile(t) // tiles, 0, proj_tile(t) % tiles, 0))
    nat = lambda w, dt: jax.ShapeDtypeStruct((batch, seq_len, w), dt)
    dil_shape = lambda dd: jax.ShapeDtypeStruct((batch, dd, seq_len // dd, B_W), bf)
    out_specs = [row(d, ffn_tile)] + [row(w, proj_tile) for w in (A_Q_W, A_KV_W, A_KV_W, B_W, B_W, B_W)]
    out_shape = [nat(d, jnp.float32), nat(A_Q_W, bf), nat(A_KV_W, bf), nat(A_KV_W, bf),
                 nat(B_W, bf), nat(B_W, bf), nat(B_W, bf)]
    for _ in range(3):
        out_specs += [dil(dd) for dd in DILATIONS]
        out_shape += [dil_shape(dd) for dd in DILATIONS]
    return pl.pallas_call(
        _ffn_in_proj_kernel,
        grid=(n_tiles + 1,),
        in_specs=[row(d, ffn_tile),
                  pl.BlockSpec((None, tm // ROPE_GROUPS, LANES),
                               lambda t: (proj_tile(t) // tiles, proj_tile(t) % tiles, 0)),
                  _const_spec(freq.shape), _const_spec(g1.shape),
                  *[_const_spec(w.shape) for w in ffn_w], _const_spec(gm.shape), _const_spec(w_in.shape)],
        out_specs=out_specs,
        out_shape=out_shape,
        scratch_shapes=[pltpu.VMEM((tm, d), jnp.bfloat16), pltpu.VMEM((tm, d), jnp.bfloat16),
                        pltpu.VMEM((tm, d), jnp.bfloat16), pltpu.VMEM((tm, d), jnp.float32),
                        pltpu.VMEM((B_W // LANES, tm, LANES), jnp.float32),
                        pltpu.VMEM((DILATIONS[0] * B_W // LANES, tm // DILATIONS[0], LANES), jnp.float32),
                        pltpu.VMEM((tm, LANES), jnp.float32), pltpu.VMEM((tm, LANES), jnp.float32),
                        pltpu.VMEM((d, A_Q_W), jnp.bfloat16)],
        compiler_params=pltpu.CompilerParams(dimension_semantics=("arbitrary",),
                                             vmem_limit_bytes=VMEM_LIMIT),
        name="ffn_in_proj",
    )(x3d, pos3d, freq, g1, *ffn_w, gm, w_in)


def _out_ffn(x1, a, b_parts, wo, g2, ffn_w, gf, final_norm):
    batch, seq_len, d = x1.shape
    tm = TOKEN_TILE
    tiles = seq_len // tm
    row = lambda w: pl.BlockSpec((None, tm, w), lambda b, i: (b, i, 0))
    dil = lambda dd: pl.BlockSpec((None, dd, tm // dd, B_W), lambda b, i: (b, 0, i, 0))
    (o1, l1), (o4, l4), (o16, l16) = b_parts
    d4, d16 = DILATIONS
    slab = lambda: pltpu.VMEM((B_W // LANES, tm, LANES), jnp.float32)
    stage = lambda: pltpu.VMEM((d4 * B_W // LANES, tm // d4, LANES), jnp.float32)
    return pl.pallas_call(
        functools.partial(_out_ffn_kernel, final_norm=final_norm),
        grid=(batch, tiles),
        in_specs=[row(d), row(A_Q_W), row(B_W), row(B_W), dil(d4), dil(d4), dil(d16), dil(d16),
                  _const_spec(wo.shape), _const_spec(g2.shape),
                  *[_const_spec(w.shape) for w in ffn_w], _const_spec(gf.shape)],
        out_specs=row(d),
        out_shape=jax.ShapeDtypeStruct((batch, seq_len, d), jnp.float32),
        scratch_shapes=[pltpu.VMEM((tm, d), jnp.bfloat16), pltpu.VMEM((tm, d), jnp.float32),
                        slab(), slab(), slab(), slab(), stage(), stage()],
        compiler_params=pltpu.CompilerParams(dimension_semantics=("arbitrary", "arbitrary"),
                                             vmem_limit_bytes=VMEM_LIMIT),
        name="out_ffn",
    )(x1, a, o1, l1, o4, l4, o16, l16, wo, g2, *ffn_w, gf)


def _banded_attention(q, k, v, sink, *, tq, halo, step_rows, emit_lse, name, casts=()):
    batch, groups, seq_len, wq = q.shape
    wkv = k.shape[-1]
    rows = min(seq_len, step_rows)
    n_seqs = step_rows // rows
    steps = seq_len // rows
    ratio = rows // halo
    halo_blocks = seq_len // halo
    tile = lambda w: pl.BlockSpec((None, n_seqs, rows, w), lambda b, r, i: (b, r, i, 0))
    prev = pl.BlockSpec((None, n_seqs, halo, wkv), lambda b, r, i: (b, r, jnp.maximum(i * ratio - 1, 0), 0))
    nxt = pl.BlockSpec((None, n_seqs, halo, wkv),
                       lambda b, r, i: (b, r, jnp.minimum((i + 1) * ratio, halo_blocks - 1), 0))
    kv_specs = [prev, tile(wkv), nxt]
    in_specs = [tile(wq), *kv_specs, *kv_specs]
    args = [q, k, k, k, v, v, v]
    if sink is not None:
        in_specs = [pl.BlockSpec(memory_space=pltpu.SMEM)] + in_specs
        args = [sink] + args
    out_specs = [tile(wq)]
    out_shape = [jax.ShapeDtypeStruct(q.shape, jnp.bfloat16)]
    if emit_lse:
        out_specs.append(tile(wq))
        out_shape.append(jax.ShapeDtypeStruct(q.shape, jnp.float32))
    grid = (batch, groups // n_seqs, steps)
    n_steps = grid[0] * grid[1] * grid[2]
    flat_step = lambda b, r, i: (b * grid[1] + r) * grid[2] + i
    for w, _, src_block in casts:
        block = (w.shape[0] // n_steps, w.shape[1])
        in_specs.append(pl.BlockSpec(block, lambda b, r, i, f=src_block: (f(flat_step(b, r, i)), 0)))
        args.append(w)
        out_specs.append(pl.BlockSpec(block, lambda b, r, i: (flat_step(b, r, i), 0)))
        out_shape.append(jax.ShapeDtypeStruct(w.shape, jnp.bfloat16))
    tk = tq + 2 * halo
    res = pl.pallas_call(
        functools.partial(_attn_kernel, seq_len=seq_len, tq=tq, halo=halo, sinks=sink is not None,
                          emit_lse=emit_lse, cast_scales=tuple(scale for _, scale, _ in casts)),
        grid=grid,
        in_specs=in_specs,
        out_specs=out_specs,
        out_shape=out_shape,
        scratch_shapes=[pltpu.VMEM((n_seqs, rows + 2 * halo, wkv), jnp.bfloat16),
                        pltpu.VMEM((n_seqs, rows + 2 * halo, wkv), jnp.bfloat16),
                        pltpu.VMEM((3, tq, tk), jnp.float32)],
        compiler_params=pltpu.CompilerParams(dimension_semantics=("arbitrary",) * 3,
                                             vmem_limit_bytes=VMEM_LIMIT),
        name=name,
    )(*args)
    return res


def kernel(x, positions, norm_ffn1, w_gate1, w_up1, w_down1, norm_mix, w_in, a_sink, w_out,
           norm_ffn2, w_gate2, w_up2, w_down2, norm_final):
    batch, seq_len, d = x.shape
    depth = norm_ffn1.shape[0]
    bf = jnp.bfloat16

    inv_freq = 1.0 / (ROPE_THETA ** (jnp.arange(0, HEAD_DIM, 2, dtype=jnp.float32) / HEAD_DIM))
    freq = jnp.tile(inv_freq, LANES // (HEAD_DIM // 2))[None, :]
    pos3d = jnp.repeat(positions.reshape(batch, seq_len // ROPE_GROUPS, ROPE_GROUPS), HEAD_DIM // 2, axis=-1)
    xc = x
    same_block = lambda s: s
    per_group = A_Q_HEADS // A_KV_HEADS
    w_out_block = lambda s: jnp.where(s < A_Q_HEADS, (s % A_KV_HEADS) * per_group + s // A_KV_HEADS, s)

    for l in range(depth):
        ffn1 = (w_gate1[l].astype(bf), w_up1[l].astype(bf), (w_down1[l] * FFN_RES_WEIGHT).astype(bf))

        outs = _ffn_in_proj(xc, pos3d, freq, norm_ffn1[l][None, :], ffn1, norm_mix[l][None, :], w_in[l].astype(bf))
        x1, aq, ak, av, bq, bk, bv = outs[:7]
        n_dil = len(DILATIONS)
        bq_d, bk_d, bv_d = outs[7:7 + n_dil], outs[7 + n_dil:7 + 2 * n_dil], outs[7 + 2 * n_dil:]

        a_out, *ffn2, wo = _banded_attention(
            aq[:, None], ak[:, None], av[:, None], a_sink[l],
            tq=2 * A_HALF_WINDOW, halo=A_HALF_WINDOW, step_rows=ATTN_STEP_ROWS, emit_lse=False, name="attn_a",
            casts=((w_gate2[l], 1, same_block), (w_up2[l], 1, same_block),
                   (w_down2[l], FFN_RES_WEIGHT, same_block), (w_out[l], 1, w_out_block)))
        assert wo.shape[0] // (batch * (seq_len // ATTN_STEP_ROWS)) == HEAD_DIM
        b_parts = []
        for window, dilation in B_PATTERNS:
            if dilation == 1:
                qkv = (bq[:, None], bk[:, None], bv[:, None])
            else:
                j = DILATIONS.index(dilation)
                qkv = (bq_d[j], bk_d[j], bv_d[j])
            halo = window // (2 * dilation)
            o, lse = _banded_attention(*qkv, None, tq=2 * halo, halo=halo, step_rows=ATTN_STEP_ROWS,
                                       emit_lse=True, name=f"attn_b_d{dilation}")
            if dilation == 1:
                o, lse = o[:, 0], lse[:, 0]
            b_parts.append((o, lse))

        xc = _out_ffn(x1, a_out[:, 0], b_parts, wo, norm_ffn2[l][None, :], ffn2,
                      norm_final[None, :], final_norm=l == depth - 1)
    return xc
```

```python
import functools

import jax
import jax.numpy as jnp
from jax import lax
from jax.experimental import pallas as pl
from jax.experimental.pallas import tpu as pltpu

HEAD_DIM = 64
A_Q_HEADS = 8
A_KV_HEADS = 2
A_HALF_WINDOW = 128
B_HEADS = 8
B_PATTERNS = ((128, 1), (512, 4), (2048, 16))
ROPE_THETA = 10000.0
NORM_EPS = 1e-6
FFN_RES_WEIGHT = 0.5

A_Q_W = A_Q_HEADS * HEAD_DIM
A_KV_W = A_KV_HEADS * HEAD_DIM
B_W = B_HEADS * HEAD_DIM

LANES = 128
NEG_BIG = -1e30
LOG2E = 1.4426950408889634
TOKEN_TILE = 512
FF_CHUNK = 256
ATTN_STEP_ROWS = 2048
V7X_VMEM_BYTES = 64 * 1024 * 1024
VMEM_LIMIT = V7X_VMEM_BYTES * 7 // 8
A_HEAD_ORDER = (0, 4, 1, 5, 2, 6, 3, 7)
DILATIONS = tuple(d for _, d in B_PATTERNS if d > 1)
ROPE_GROUPS = LANES // (HEAD_DIM // 2)


def _rms_norm(x, g):
    y = x * lax.rsqrt(jnp.mean(x * x, axis=-1, keepdims=True) + NORM_EPS)
    return y * g


def _swiglu_accumulate(h_ref, wg_ref, wu_ref, wd_ref, acc_ref, before_chunk=None):
    d_ff = wg_ref.shape[1]
    for c in range(d_ff // FF_CHUNK):
        if before_chunk and c in before_chunk:
            before_chunk[c]()
        sl = slice(c * FF_CHUNK, (c + 1) * FF_CHUNK)
        g = jnp.dot(h_ref[...], wg_ref[:, sl], preferred_element_type=jnp.float32)
        u = jnp.dot(h_ref[...], wu_ref[:, sl], preferred_element_type=jnp.float32)
        a = (g * jax.nn.sigmoid(g) * u).astype(jnp.bfloat16)
        down = jnp.dot(a, wd_ref[sl, :], preferred_element_type=jnp.float32)
        if c == 0:
            acc_ref[...] = down
        else:
            acc_ref[...] += down


def _rope(t, cos, sin_signed, first_half):
    parts = []
    for j in range(t.shape[1] // LANES):
        tj = t[:, j * LANES:(j + 1) * LANES]
        partner = jnp.where(first_half, pltpu.roll(tj, LANES - HEAD_DIM // 2, 1), pltpu.roll(tj, HEAD_DIM // 2, 1))
        parts.append(tj * cos + partner * sin_signed)
    return parts[0] if len(parts) == 1 else jnp.concatenate(parts, axis=1)


def _write_deinterleaved(val, slab_ref, stage_ref, out_refs):
    tm, width = val.shape
    n_slabs = width // LANES
    d4, d16 = DILATIONS
    ratio = d16 // d4
    rows4 = tm // d4
    for c in range(n_slabs):
        slab_ref[c] = val[:, c * LANES:(c + 1) * LANES]
    for r in range(d4):
        for c in range(n_slabs):
            part = slab_ref[c, pl.ds(r, rows4, stride=d4), :]
            out_refs[d4][r, :, c * LANES:(c + 1) * LANES] = part.astype(out_refs[d4].dtype)
            stage_ref[r * n_slabs + c] = part
    for r in range(d4):
        for q in range(ratio):
            for c in range(n_slabs):
                part = stage_ref[r * n_slabs + c, pl.ds(q, rows4 // ratio, stride=ratio), :]
                out_refs[d16][q * d4 + r, :, c * LANES:(c + 1) * LANES] = part.astype(out_refs[d16].dtype)


def _ffn_in_proj_kernel(x_ref, pos_ref, freq_ref, g1_ref, wg_ref, wu_ref, wd_ref, gm_ref, win_ref,
                        x1_ref, aq_ref, ak_ref, av_ref, bq_ref, bk_ref, bv_ref, *rest):
    n_dil = len(DILATIONS)
    bq_d = dict(zip(DILATIONS, rest[0:n_dil]))
    bk_d = dict(zip(DILATIONS, rest[n_dil:2 * n_dil]))
    bv_d = dict(zip(DILATIONS, rest[2 * n_dil:3 * n_dil]))
    (h_ref, h2_cur_ref, h2_next_ref, acc_ref, slab_ref, stage_ref, cos_ref, sin_ref, waq_ref,
     pos_b_ref) = rest[3 * n_dil:]

    @pl.when(pl.program_id(0) == 0)
    def _():
        h2_next_ref[...] = jnp.zeros_like(h2_next_ref)
        in_low = lax.broadcasted_iota(jnp.int32, (win_ref.shape[0], LANES), 1) < HEAD_DIM
        per_group = A_Q_HEADS // A_KV_HEADS
        for c in range(A_Q_W // LANES):
            halves = []
            for dst_half, head in enumerate((c, c + per_group)):
                lo = (head // 2) * LANES
                src = win_ref[:, lo:lo + LANES].astype(jnp.float32)
                halves.append(src if head % 2 == dst_half else pltpu.roll(src, HEAD_DIM, 1))
            waq_ref[:, c * LANES:(c + 1) * LANES] = jnp.where(in_low, halves[0], halves[1]).astype(waq_ref.dtype)

    h2_cur_ref[...] = h2_next_ref[...]

    half = HEAD_DIM // 2
    tm = cos_ref.shape[0]
    rows_d = tm // ROPE_GROUPS
    lane_d = lax.broadcasted_iota(jnp.int32, (rows_d, LANES), 1)
    pos_b_ref[...] = jnp.broadcast_to(pos_ref[...], pos_b_ref.shape)
    pos_d = pos_b_ref[pl.ds(0, rows_d, stride=ROPE_GROUPS), :]
    for k in range(1, ROPE_GROUPS):
        pos_d = jnp.where((lane_d // half) == k, pos_b_ref[pl.ds(k, rows_d, stride=ROPE_GROUPS), :], pos_d)
    ang = pos_d.astype(jnp.float32) * freq_ref[...]
    first_half_d = (lane_d % HEAD_DIM) < half
    tables = ((jnp.cos(ang), cos_ref, False), (jnp.sin(ang), sin_ref, True))
    for k in range(ROPE_GROUPS):
        in_group = (lane_d // half) == k
        for table, ref, signed in tables:
            one = jnp.where(in_group, table, 0.0)
            full = one + pltpu.roll(one, half, 1)
            full = full + pltpu.roll(full, 2 * half, 1)
            if signed:
                full = jnp.where(first_half_d, -full, full)
            ref[pl.ds(k, tm // ROPE_GROUPS, stride=ROPE_GROUPS), :] = full
    cos = cos_ref[...]
    sin_signed = sin_ref[...]
    lane = lax.broadcasted_iota(jnp.int32, cos.shape, 1)
    first_half = (lane % HEAD_DIM) < half
    scale = HEAD_DIM ** -0.5 * LOG2E

    def proj(src_ref, lo, width):
        return jnp.dot(src_ref[...], win_ref[:, lo:lo + width], preferred_element_type=jnp.float32)

    offs = {}
    o = 0
    for name, width in (("aq", A_Q_W), ("ak", A_KV_W), ("av", A_KV_W), ("bq", B_W), ("bk", B_W), ("bv", B_W)):
        offs[name] = (o, width)
        o += width
    rope = lambda t: _rope(t, cos, sin_signed, first_half)

    def emit_bv():
        bv = proj(h2_next_ref, *offs["bv"])
        bv_ref[...] = bv.astype(bv_ref.dtype)
        _write_deinterleaved(bv, slab_ref, stage_ref, bv_d)

    def emit_bk():
        bk = rope(proj(h2_next_ref, *offs["bk"]))
        bk_ref[...] = bk.astype(bk_ref.dtype)
        _write_deinterleaved(bk, slab_ref, stage_ref, bk_d)

    def emit_bq():
        bq = rope(proj(h2_next_ref, *offs["bq"])) * scale
        bq_ref[...] = bq.astype(bq_ref.dtype)
        _write_deinterleaved(bq, slab_ref, stage_ref, bq_d)

    x = x_ref[...]
    h_ref[...] = _rms_norm(x, g1_ref[...]).astype(jnp.bfloat16)
    _swiglu_accumulate(h_ref, wg_ref, wu_ref, wd_ref, acc_ref, before_chunk={0: emit_bv, 3: emit_bk, 6: emit_bq})
    x1 = x + acc_ref[...]
    x1_ref[...] = x1
    h2_next_ref[...] = _rms_norm(x1, gm_ref[...]).astype(jnp.bfloat16)

    aq = jnp.dot(h2_cur_ref[...], waq_ref[...], preferred_element_type=jnp.float32)
    aq_ref[...] = (rope(aq) * scale).astype(aq_ref.dtype)
    ak_ref[...] = rope(proj(h2_cur_ref, *offs["ak"])).astype(ak_ref.dtype)
    av_ref[...] = proj(h2_cur_ref, *offs["av"]).astype(av_ref.dtype)


def _interleave_into(slab_ref, blk_ref, d, stage_ref=None):
    n_slabs = slab_ref.shape[0]
    rows = blk_ref.shape[1]
    if stage_ref is None:
        for r in range(d):
            for c in range(n_slabs):
                slab_ref[c, pl.ds(r, rows, stride=d), :] = blk_ref[r, :, c * LANES:(c + 1) * LANES].astype(jnp.float32)
        return
    d4 = DILATIONS[0]
    ratio = d // d4
    for r in range(d4):
        for c in range(n_slabs):
            for q in range(ratio):
                part = blk_ref[q * d4 + r, :, c * LANES:(c + 1) * LANES].astype(jnp.float32)
                stage_ref[r * n_slabs + c, pl.ds(q, rows, stride=ratio), :] = part
            slab_ref[c, pl.ds(r, rows * ratio, stride=d4), :] = stage_ref[r * n_slabs + c]


def _out_ffn_kernel(x1_ref, a_ref, o1_ref, l1_ref, o4_ref, l4_ref, o16_ref, l16_ref,
                    wo_ref, g2_ref, wg_ref, wu_ref, wd_ref, gf_ref,
                    out_ref, h_ref, acc_ref, so4_ref, sl4_ref, so16_ref, sl16_ref, stage_o_ref, stage_l_ref,
                    *, final_norm):
    d4, d16 = DILATIONS
    _interleave_into(so4_ref, o4_ref, d4)
    _interleave_into(sl4_ref, l4_ref, d4)
    _interleave_into(so16_ref, o16_ref, d16, stage_o_ref)
    _interleave_into(sl16_ref, l16_ref, d16, stage_l_ref)

    mix = jnp.dot(a_ref[...], wo_ref[:A_Q_W, :], preferred_element_type=jnp.float32)
    for c in range(B_W // LANES):
        sl = slice(c * LANES, (c + 1) * LANES)
        la, lb, lc = l1_ref[:, sl], sl4_ref[c], sl16_ref[c]
        m = jnp.maximum(jnp.maximum(la, lb), lc)
        wa, wb, wc = jnp.exp2(la - m), jnp.exp2(lb - m), jnp.exp2(lc - m)
        num = wa * o1_ref[:, sl].astype(jnp.float32) + wb * so4_ref[c] + wc * so16_ref[c]
        b = (num / (wa + wb + wc)).astype(jnp.bfloat16)
        mix += jnp.dot(b, wo_ref[A_Q_W + c * LANES:A_Q_W + (c + 1) * LANES, :], preferred_element_type=jnp.float32)
    x2 = x1_ref[...] + mix
    h_ref[...] = _rms_norm(x2, g2_ref[...]).astype(jnp.bfloat16)
    _swiglu_accumulate(h_ref, wg_ref, wu_ref, wd_ref, acc_ref)
    x3 = x2 + acc_ref[...]
    out_ref[...] = _rms_norm(x3, gf_ref[...]) if final_norm else x3


def _attn_kernel(*refs, seq_len, tq, halo, sinks, emit_lse, cast_scales):
    if sinks:
        sink_ref, refs = refs[0], refs[1:]
    q_ref, kp_ref, kc_ref, kn_ref, vp_ref, vc_ref, vn_ref = refs[:7]
    refs = refs[7:]
    cast_in, refs = refs[:len(cast_scales)], refs[len(cast_scales):]
    o_ref, refs = refs[0], refs[1:]
    if emit_lse:
        lse_ref, refs = refs[0], refs[1:]
    cast_out, refs = refs[:len(cast_scales)], refs[len(cast_scales):]
    kbuf, vbuf, bias_ref = refs

    for src, dst, scale in zip(cast_in, cast_out, cast_scales):
        dst[...] = (src[...] if scale == 1 else src[...] * scale).astype(dst.dtype)

    n_seqs, rows = q_ref.shape[:2]
    tk = tq + 2 * halo
    tiles = rows // tq
    tiles_in_seq = seq_len // tq
    kv_tiles = kbuf.shape[2] // LANES
    q_tiles = q_ref.shape[2] // LANES
    step = pl.program_id(2)

    @pl.when((pl.program_id(0) == 0) & (pl.program_id(1) == 0) & (step == 0))
    def _():
        row = lax.broadcasted_iota(jnp.int32, (tq, tk), 0)
        col = lax.broadcasted_iota(jnp.int32, (tq, tk), 1)
        band = jnp.abs(row + halo - col) <= halo
        bias_ref[0] = jnp.where(band & (col >= halo), 0.0, NEG_BIG)
        bias_ref[1] = jnp.where(band, 0.0, NEG_BIG)
        bias_ref[2] = jnp.where(band & (col < tq + halo), 0.0, NEG_BIG)

    kbuf[:, 0:halo, :] = kp_ref[...]
    kbuf[:, halo:halo + rows, :] = kc_ref[...]
    kbuf[:, halo + rows:, :] = kn_ref[...]
    vbuf[:, 0:halo, :] = vp_ref[...]
    vbuf[:, halo:halo + rows, :] = vc_ref[...]
    vbuf[:, halo + rows:, :] = vn_ref[...]

    lane = lax.broadcasted_iota(jnp.int32, (tq, LANES), 1)
    low = lane < HEAD_DIM
    ones = jnp.ones((tk, LANES), jnp.bfloat16)
    zero_q = jnp.zeros((tq, LANES), jnp.bfloat16)

    def scores(n, t, c):
        kc = c if kv_tiles == q_tiles else 0
        r0 = t * tq
        qp = q_ref[n, r0:r0 + tq, c * LANES:(c + 1) * LANES]
        q2 = jnp.concatenate([jnp.where(low, qp, zero_q), jnp.where(low, zero_q, qp)], axis=0)
        kp = kbuf[n, r0:r0 + tk, kc * LANES:(kc + 1) * LANES]
        return lax.dot_general(q2, kp, (((1,), (1,)), ((), ())), preferred_element_type=jnp.float32)

    def finish(n, t, c, s):
        kc = c if kv_tiles == q_tiles else 0
        r0 = t * tq
        g = step * tiles + t
        bias = bias_ref[jnp.where(g == 0, 0, jnp.where(g == tiles_in_seq - 1, 2, 1))]
        vext = jnp.concatenate([vbuf[n, r0:r0 + tk, kc * LANES:(kc + 1) * LANES], ones], axis=1)
        s_lo = s[:tq] + bias
        s_hi = s[tq:] + bias
        m_lo = jnp.max(s_lo, axis=-1, keepdims=True)
        m_hi = jnp.max(s_hi, axis=-1, keepdims=True)
        if sinks:
            sink_lo = sink_ref[A_HEAD_ORDER[2 * c]] * LOG2E
            sink_hi = sink_ref[A_HEAD_ORDER[2 * c + 1]] * LOG2E
            m_lo = jnp.maximum(m_lo, sink_lo)
            m_hi = jnp.maximum(m_hi, sink_hi)
        p = jnp.concatenate([jnp.exp2(s_lo - m_lo), jnp.exp2(s_hi - m_hi)], axis=0).astype(jnp.bfloat16)
        o2 = jnp.dot(p, vext, preferred_element_type=jnp.float32)
        num = jnp.where(low, o2[:tq, :LANES], o2[tq:, :LANES])
        den = jnp.where(low, o2[:tq, LANES:], o2[tq:, LANES:])
        m = jnp.where(low, m_lo, m_hi)
        if sinks:
            den = den + jnp.exp2(jnp.where(low, sink_lo, sink_hi) - m)
        o_ref[n, r0:r0 + tq, c * LANES:(c + 1) * LANES] = (num / den).astype(o_ref.dtype)
        if emit_lse:
            lse_ref[n, r0:r0 + tq, c * LANES:(c + 1) * LANES] = m + jnp.log(den) * LOG2E

    for n, t, c in [(n, t, c) for n in range(n_seqs) for t in range(tiles) for c in range(q_tiles)]:
        finish(n, t, c, scores(n, t, c))


def _const_spec(shape):
    zeros = (0,) * len(shape)
    return pl.BlockSpec(shape, lambda *_: zeros, pipeline_mode=pl.Buffered(1))


def _ffn_in_proj(x3d, pos3d, freq, g1, ffn_w, gm, w_in):
    batch, seq_len, d = x3d.shape
    tm = TOKEN_TILE
    tiles = seq_len // tm
    bf = jnp.bfloat16
    n_tiles = batch * tiles
    ffn_tile = lambda t: jnp.minimum(t, n_tiles - 1)
    proj_tile = lambda t: jnp.maximum(t - 1, 0)
    row = lambda w, which: pl.BlockSpec((None, tm, w), lambda t: (which(t) // tiles, which(t) % tiles, 0))
    dil = lambda dd: pl.BlockSpec((None, dd, tm // dd, B_W),
                                  lambda t: (proj_tile(t) // tiles, 0, proj_tile(t) % tiles, 0))
    nat = lambda w, dt: jax.ShapeDtypeStruct((batch, seq_len, w), dt)
    dil_shape = lambda dd: jax.ShapeDtypeStruct((batch, dd, seq_len // dd, B_W), bf)
    out_specs = [row(d, ffn_tile)] + [row(w, proj_tile) for w in (A_Q_W, A_KV_W, A_KV_W, B_W, B_W, B_W)]
    out_shape = [nat(d, jnp.float32), nat(A_Q_W, bf), nat(A_KV_W, bf), nat(A_KV_W, bf),
                 nat(B_W, bf), nat(B_W, bf), nat(B_W, bf)]
    for _ in range(3):
        out_specs += [dil(dd) for dd in DILATIONS]
        out_shape += [dil_shape(dd) for dd in DILATIONS]
    return pl.pallas_call(
        _ffn_in_proj_kernel,
        grid=(n_tiles + 1,),
        in_specs=[row(d, ffn_tile), row(1, proj_tile), _const_spec(freq.shape), _const_spec(g1.shape),
                  *[_const_spec(w.shape) for w in ffn_w], _const_spec(gm.shape), _const_spec(w_in.shape)],
        out_specs=out_specs,
        out_shape=out_shape,
        scratch_shapes=[pltpu.VMEM((tm, d), jnp.bfloat16), pltpu.VMEM((tm, d), jnp.bfloat16),
                        pltpu.VMEM((tm, d), jnp.bfloat16), pltpu.VMEM((tm, d), jnp.float32),
                        pltpu.VMEM((B_W // LANES, tm, LANES), jnp.float32),
                        pltpu.VMEM((DILATIONS[0] * B_W // LANES, tm // DILATIONS[0], LANES), jnp.float32),
                        pltpu.VMEM((tm, LANES), jnp.float32), pltpu.VMEM((tm, LANES), jnp.float32),
                        pltpu.VMEM((d, A_Q_W), jnp.bfloat16), pltpu.VMEM((tm, LANES), jnp.int32)],
        compiler_params=pltpu.CompilerParams(dimension_semantics=("arbitrary",),
                                             vmem_limit_bytes=VMEM_LIMIT),
        name="ffn_in_proj",
    )(x3d, pos3d, freq, g1, *ffn_w, gm, w_in)


def _out_ffn(x1, a, b_parts, wo, g2, ffn_w, gf, final_norm):
    batch, seq_len, d = x1.shape
    tm = TOKEN_TILE
    tiles = seq_len // tm
    row = lambda w: pl.BlockSpec((None, tm, w), lambda b, i: (b, i, 0))
    dil = lambda dd: pl.BlockSpec((None, dd, tm // dd, B_W), lambda b, i: (b, 0, i, 0))
    (o1, l1), (o4, l4), (o16, l16) = b_parts
    d4, d16 = DILATIONS
    slab = lambda: pltpu.VMEM((B_W // LANES, tm, LANES), jnp.float32)
    stage = lambda: pltpu.VMEM((d4 * B_W // LANES, tm // d4, LANES), jnp.float32)
    return pl.pallas_call(
        functools.partial(_out_ffn_kernel, final_norm=final_norm),
        grid=(batch, tiles),
        in_specs=[row(d), row(A_Q_W), row(B_W), row(B_W), dil(d4), dil(d4), dil(d16), dil(d16),
                  _const_spec(wo.shape), _const_spec(g2.shape),
                  *[_const_spec(w.shape) for w in ffn_w], _const_spec(gf.shape)],
        out_specs=row(d),
        out_shape=jax.ShapeDtypeStruct((batch, seq_len, d), jnp.float32),
        scratch_shapes=[pltpu.VMEM((tm, d), jnp.bfloat16), pltpu.VMEM((tm, d), jnp.float32),
                        slab(), slab(), slab(), slab(), stage(), stage()],
        compiler_params=pltpu.CompilerParams(dimension_semantics=("arbitrary", "arbitrary"),
                                             vmem_limit_bytes=VMEM_LIMIT),
        name="out_ffn",
    )(x1, a, o1, l1, o4, l4, o16, l16, wo, g2, *ffn_w, gf)


def _banded_attention(q, k, v, sink, *, tq, halo, step_rows, emit_lse, name, casts=()):
    batch, groups, seq_len, wq = q.shape
    wkv = k.shape[-1]
    rows = min(seq_len, step_rows)
    n_seqs = step_rows // rows
    steps = seq_len // rows
    ratio = rows // halo
    halo_blocks = seq_len // halo
    tile = lambda w: pl.BlockSpec((None, n_seqs, rows, w), lambda b, r, i: (b, r, i, 0))
    prev = pl.BlockSpec((None, n_seqs, halo, wkv), lambda b, r, i: (b, r, jnp.maximum(i * ratio - 1, 0), 0))
    nxt = pl.BlockSpec((None, n_seqs, halo, wkv),
                       lambda b, r, i: (b, r, jnp.minimum((i + 1) * ratio, halo_blocks - 1), 0))
    kv_specs = [prev, tile(wkv), nxt]
    in_specs = [tile(wq), *kv_specs, *kv_specs]
    args = [q, k, k, k, v, v, v]
    if sink is not None:
        in_specs = [pl.BlockSpec(memory_space=pltpu.SMEM)] + in_specs
        args = [sink] + args
    out_specs = [tile(wq)]
    out_shape = [jax.ShapeDtypeStruct(q.shape, jnp.bfloat16)]
    if emit_lse:
        out_specs.append(tile(wq))
        out_shape.append(jax.ShapeDtypeStruct(q.shape, jnp.float32))
    grid = (batch, groups // n_seqs, steps)
    n_steps = grid[0] * grid[1] * grid[2]
    flat_step = lambda b, r, i: (b * grid[1] + r) * grid[2] + i
    for w, _, src_block in casts:
        block = (w.shape[0] // n_steps, w.shape[1])
        in_specs.append(pl.BlockSpec(block, lambda b, r, i, f=src_block: (f(flat_step(b, r, i)), 0)))
        args.append(w)
        out_specs.append(pl.BlockSpec(block, lambda b, r, i: (flat_step(b, r, i), 0)))
        out_shape.append(jax.ShapeDtypeStruct(w.shape, jnp.bfloat16))
    tk = tq + 2 * halo
    res = pl.pallas_call(
        functools.partial(_attn_kernel, seq_len=seq_len, tq=tq, halo=halo, sinks=sink is not None,
                          emit_lse=emit_lse, cast_scales=tuple(scale for _, scale, _ in casts)),
        grid=grid,
        in_specs=in_specs,
        out_specs=out_specs,
        out_shape=out_shape,
        scratch_shapes=[pltpu.VMEM((n_seqs, rows + 2 * halo, wkv), jnp.bfloat16),
                        pltpu.VMEM((n_seqs, rows + 2 * halo, wkv), jnp.bfloat16),
                        pltpu.VMEM((3, tq, tk), jnp.float32)],
        compiler_params=pltpu.CompilerParams(dimension_semantics=("arbitrary",) * 3,
                                             vmem_limit_bytes=VMEM_LIMIT),
        name=name,
    )(*args)
    return res


def kernel(x, positions, norm_ffn1, w_gate1, w_up1, w_down1, norm_mix, w_in, a_sink, w_out,
           norm_ffn2, w_gate2, w_up2, w_down2, norm_final):
    batch, seq_len, d = x.shape
    depth = norm_ffn1.shape[0]
    bf = jnp.bfloat16

    inv_freq = 1.0 / (ROPE_THETA ** (jnp.arange(0, HEAD_DIM, 2, dtype=jnp.float32) / HEAD_DIM))
    freq = jnp.tile(inv_freq, LANES // (HEAD_DIM // 2))[None, :]
    pos3d = positions[:, :, None]
    xc = x
    same_block = lambda s: s
    per_group = A_Q_HEADS // A_KV_HEADS
    w_out_block = lambda s: jnp.where(s < A_Q_HEADS, (s % A_KV_HEADS) * per_group + s // A_KV_HEADS, s)

    for l in range(depth):
        ffn1 = (w_gate1[l].astype(bf), w_up1[l].astype(bf), (w_down1[l] * FFN_RES_WEIGHT).astype(bf))

        outs = _ffn_in_proj(xc, pos3d, freq, norm_ffn1[l][None, :], ffn1, norm_mix[l][None, :], w_in[l].astype(bf))
        x1, aq, ak, av, bq, bk, bv = outs[:7]
        n_dil = len(DILATIONS)
        bq_d, bk_d, bv_d = outs[7:7 + n_dil], outs[7 + n_dil:7 + 2 * n_dil], outs[7 + 2 * n_dil:]

        a_out, *ffn2, wo = _banded_attention(
            aq[:, None], ak[:, None], av[:, None], a_sink[l],
            tq=2 * A_HALF_WINDOW, halo=A_HALF_WINDOW, step_rows=ATTN_STEP_ROWS, emit_lse=False, name="attn_a",
            casts=((w_gate2[l], 1, same_block), (w_up2[l], 1, same_block),
                   (w_down2[l], FFN_RES_WEIGHT, same_block), (w_out[l], 1, w_out_block)))
        assert wo.shape[0] // (batch * (seq_len // ATTN_STEP_ROWS)) == HEAD_DIM
        b_parts = []
        for window, dilation in B_PATTERNS:
            if dilation == 1:
                qkv = (bq[:, None], bk[:, None], bv[:, None])
            else:
                j = DILATIONS.index(dilation)
                qkv = (bq_d[j], bk_d[j], bv_d[j])
            halo = window // (2 * dilation)
            o, lse = _banded_attention(*qkv, None, tq=2 * halo, halo=halo, step_rows=ATTN_STEP_ROWS,
                                       emit_lse=True, name=f"attn_b_d{dilation}")
            if dilation == 1:
                o, lse = o[:, 0], lse[:, 0]
            b_parts.append((o, lse))

        xc = _out_ffn(x1, a_out[:, 0], b_parts, wo, norm_ffn2[l][None, :], ffn2,
                      norm_final[None, :], final_norm=l == depth - 1)
    return xc
```

```python
import functools

import jax
import jax.numpy as jnp
from jax import lax
from jax.experimental import pallas as pl
from jax.experimental.pallas import tpu as pltpu

HEAD_DIM = 64
A_Q_HEADS = 8
A_KV_HEADS = 2
A_HALF_WINDOW = 128
B_HEADS = 8
B_PATTERNS = ((128, 1), (512, 4), (2048, 16))
ROPE_THETA = 10000.0
NORM_EPS = 1e-6
FFN_RES_WEIGHT = 0.5

A_Q_W = A_Q_HEADS * HEAD_DIM
A_KV_W = A_KV_HEADS * HEAD_DIM
B_W = B_HEADS * HEAD_DIM

LANES = 128
NEG_BIG = -1e30
LOG2E = 1.4426950408889634
TOKEN_TILE = 512
FF_CHUNK = 256
ATTN_STEP_ROWS = 2048
V7X_VMEM_BYTES = 64 * 1024 * 1024
VMEM_LIMIT = V7X_VMEM_BYTES * 7 // 8
A_HEAD_ORDER = (0, 4, 1, 5, 2, 6, 3, 7)
DILATIONS = tuple(d for _, d in B_PATTERNS if d > 1)
ROPE_GROUPS = LANES // (HEAD_DIM // 2)


def _rms_norm(x, g):
    y = x * lax.rsqrt(jnp.mean(x * x, axis=-1, keepdims=True) + NORM_EPS)
    return y * g


def _swiglu_accumulate(h_ref, wg_ref, wu_ref, wd_ref, acc_ref, before_chunk=None):
    d_ff = wg_ref.shape[1]
    for c in range(d_ff // FF_CHUNK):
        if before_chunk and c in before_chunk:
            before_chunk[c]()
        sl = slice(c * FF_CHUNK, (c + 1) * FF_CHUNK)
        g = jnp.dot(h_ref[...], wg_ref[:, sl], preferred_element_type=jnp.float32)
        u = jnp.dot(h_ref[...], wu_ref[:, sl], preferred_element_type=jnp.float32)
        a = (g * jax.nn.sigmoid(g) * u).astype(jnp.bfloat16)
        down = jnp.dot(a, wd_ref[sl, :], preferred_element_type=jnp.float32)
        if c == 0:
            acc_ref[...] = down
        else:
            acc_ref[...] += down


def _rope(t, cos, sin_signed, first_half):
    parts = []
    for j in range(t.shape[1] // LANES):
        tj = t[:, j * LANES:(j + 1) * LANES]
        partner = jnp.where(first_half, pltpu.roll(tj, LANES - HEAD_DIM // 2, 1), pltpu.roll(tj, HEAD_DIM // 2, 1))
        parts.append(tj * cos + partner * sin_signed)
    return parts[0] if len(parts) == 1 else jnp.concatenate(parts, axis=1)


def _write_deinterleaved(val, slab_ref, stage_ref, out_refs):
    tm, width = val.shape
    n_slabs = width // LANES
    d4, d16 = DILATIONS
    ratio = d16 // d4
    rows4 = tm // d4
    for c in range(n_slabs):
        slab_ref[c] = val[:, c * LANES:(c + 1) * LANES]
    for r in range(d4):
        for c in range(n_slabs):
            part = slab_ref[c, pl.ds(r, rows4, stride=d4), :]
            out_refs[d4][r, :, c * LANES:(c + 1) * LANES] = part.astype(out_refs[d4].dtype)
            stage_ref[r * n_slabs + c] = part
    for r in range(d4):
        for q in range(ratio):
            for c in range(n_slabs):
                part = stage_ref[r * n_slabs + c, pl.ds(q, rows4 // ratio, stride=ratio), :]
                out_refs[d16][q * d4 + r, :, c * LANES:(c + 1) * LANES] = part.astype(out_refs[d16].dtype)


def _ffn_in_proj_kernel(x_ref, pos_ref, freq_ref, g1_ref, wg_ref, wu_ref, wd_ref, gm_ref, win_ref,
                        x1_ref, aq_ref, ak_ref, av_ref, bq_ref, bk_ref, bv_ref, *rest):
    n_dil = len(DILATIONS)
    bq_d = dict(zip(DILATIONS, rest[0:n_dil]))
    bk_d = dict(zip(DILATIONS, rest[n_dil:2 * n_dil]))
    bv_d = dict(zip(DILATIONS, rest[2 * n_dil:3 * n_dil]))
    h_ref, h2_cur_ref, h2_next_ref, acc_ref, slab_ref, stage_ref, cos_ref, sin_ref, waq_ref = rest[3 * n_dil:]

    @pl.when(pl.program_id(0) == 0)
    def _():
        h2_next_ref[...] = jnp.zeros_like(h2_next_ref)
        in_low = lax.broadcasted_iota(jnp.int32, (win_ref.shape[0], LANES), 1) < HEAD_DIM
        per_group = A_Q_HEADS // A_KV_HEADS
        for c in range(A_Q_W // LANES):
            halves = []
            for dst_half, head in enumerate((c, c + per_group)):
                lo = (head // 2) * LANES
                src = win_ref[:, lo:lo + LANES].astype(jnp.float32)
                halves.append(src if head % 2 == dst_half else pltpu.roll(src, HEAD_DIM, 1))
            waq_ref[:, c * LANES:(c + 1) * LANES] = jnp.where(in_low, halves[0], halves[1]).astype(waq_ref.dtype)

    h2_cur_ref[...] = h2_next_ref[...]

    half = HEAD_DIM // 2
    tm = cos_ref.shape[0]
    ang = pos_ref[...].astype(jnp.float32) * freq_ref[...]
    lane_d = lax.broadcasted_iota(jnp.int32, ang.shape, 1)
    first_half_d = (lane_d % HEAD_DIM) < half
    tables = ((jnp.cos(ang), cos_ref, False), (jnp.sin(ang), sin_ref, True))
    for k in range(ROPE_GROUPS):
        in_group = (lane_d // half) == k
        for table, ref, signed in tables:
            one = jnp.where(in_group, table, 0.0)
            full = one + pltpu.roll(one, half, 1)
            full = full + pltpu.roll(full, 2 * half, 1)
            if signed:
                full = jnp.where(first_half_d, -full, full)
            ref[pl.ds(k, tm // ROPE_GROUPS, stride=ROPE_GROUPS), :] = full
    cos = cos_ref[...]
    sin_signed = sin_ref[...]
    lane = lax.broadcasted_iota(jnp.int32, cos.shape, 1)
    first_half = (lane % HEAD_DIM) < half
    scale = HEAD_DIM ** -0.5 * LOG2E

    def proj(src_ref, lo, width):
        return jnp.dot(src_ref[...], win_ref[:, lo:lo + width], preferred_element_type=jnp.float32)

    offs = {}
    o = 0
    for name, width in (("aq", A_Q_W), ("ak", A_KV_W), ("av", A_KV_W), ("bq", B_W), ("bk", B_W), ("bv", B_W)):
        offs[name] = (o, width)
        o += width
    rope = lambda t: _rope(t, cos, sin_signed, first_half)

    def emit_bv():
        bv = proj(h2_next_ref, *offs["bv"])
        bv_ref[...] = bv.astype(bv_ref.dtype)
        _write_deinterleaved(bv, slab_ref, stage_ref, bv_d)

    def emit_bk():
        bk = rope(proj(h2_next_ref, *offs["bk"]))
        bk_ref[...] = bk.astype(bk_ref.dtype)
        _write_deinterleaved(bk, slab_ref, stage_ref, bk_d)

    def emit_bq():
        bq = rope(proj(h2_next_ref, *offs["bq"])) * scale
        bq_ref[...] = bq.astype(bq_ref.dtype)
        _write_deinterleaved(bq, slab_ref, stage_ref, bq_d)

    x = x_ref[...]
    h_ref[...] = _rms_norm(x, g1_ref[...]).astype(jnp.bfloat16)
    _swiglu_accumulate(h_ref, wg_ref, wu_ref, wd_ref, acc_ref, before_chunk={0: emit_bv, 3: emit_bk, 6: emit_bq})
    x1 = x + acc_ref[...]
    x1_ref[...] = x1
    h2_next_ref[...] = _rms_norm(x1, gm_ref[...]).astype(jnp.bfloat16)

    aq = jnp.dot(h2_cur_ref[...], waq_ref[...], preferred_element_type=jnp.float32)
    aq_ref[...] = (rope(aq) * scale).astype(aq_ref.dtype)
    ak_ref[...] = rope(proj(h2_cur_ref, *offs["ak"])).astype(ak_ref.dtype)
    av_ref[...] = proj(h2_cur_ref, *offs["av"]).astype(av_ref.dtype)


def _interleave_into(slab_ref, blk_ref, d, stage_ref=None):
    n_slabs = slab_ref.shape[0]
    rows = blk_ref.shape[1]
    if stage_ref is None:
        for r in range(d):
            for c in range(n_slabs):
                slab_ref[c, pl.ds(r, rows, stride=d), :] = blk_ref[r, :, c * LANES:(c + 1) * LANES].astype(jnp.float32)
        return
    d4 = DILATIONS[0]
    ratio = d // d4
    for r in range(d4):
        for c in range(n_slabs):
            for q in range(ratio):
                part = blk_ref[q * d4 + r, :, c * LANES:(c + 1) * LANES].astype(jnp.float32)
                stage_ref[r * n_slabs + c, pl.ds(q, rows, stride=ratio), :] = part
            slab_ref[c, pl.ds(r, rows * ratio, stride=d4), :] = stage_ref[r * n_slabs + c]


def _out_ffn_kernel(x1_ref, a_ref, o1_ref, l1_ref, o4_ref, l4_ref, o16_ref, l16_ref,
                    wo_ref, g2_ref, wg_ref, wu_ref, wd_ref, gf_ref,
                    out_ref, h_ref, acc_ref, so4_ref, sl4_ref, so16_ref, sl16_ref, stage_o_ref, stage_l_ref,
                    *, final_norm):
    d4, d16 = DILATIONS
    _interleave_into(so4_ref, o4_ref, d4)
    _interleave_into(sl4_ref, l4_ref, d4)
    _interleave_into(so16_ref, o16_ref, d16, stage_o_ref)
    _interleave_into(sl16_ref, l16_ref, d16, stage_l_ref)

    mix = jnp.dot(a_ref[...], wo_ref[:A_Q_W, :], preferred_element_type=jnp.float32)
    for c in range(B_W // LANES):
        sl = slice(c * LANES, (c + 1) * LANES)
        la, lb, lc = l1_ref[:, sl], sl4_ref[c], sl16_ref[c]
        m = jnp.maximum(jnp.maximum(la, lb), lc)
        wa, wb, wc = jnp.exp2(la - m), jnp.exp2(lb - m), jnp.exp2(lc - m)
        num = wa * o1_ref[:, sl].astype(jnp.float32) + wb * so4_ref[c] + wc * so16_ref[c]
        b = (num / (wa + wb + wc)).astype(jnp.bfloat16)
        mix += jnp.dot(b, wo_ref[A_Q_W + c * LANES:A_Q_W + (c + 1) * LANES, :], preferred_element_type=jnp.float32)
    x2 = x1_ref[...] + mix
    h_ref[...] = _rms_norm(x2, g2_ref[...]).astype(jnp.bfloat16)
    _swiglu_accumulate(h_ref, wg_ref, wu_ref, wd_ref, acc_ref)
    x3 = x2 + acc_ref[...]
    out_ref[...] = _rms_norm(x3, gf_ref[...]) if final_norm else x3


def _attn_kernel(*refs, seq_len, tq, halo, sinks, emit_lse, cast_scales):
    if sinks:
        sink_ref, refs = refs[0], refs[1:]
    q_ref, kp_ref, kc_ref, kn_ref, vp_ref, vc_ref, vn_ref = refs[:7]
    refs = refs[7:]
    cast_in, refs = refs[:len(cast_scales)], refs[len(cast_scales):]
    o_ref, refs = refs[0], refs[1:]
    if emit_lse:
        lse_ref, refs = refs[0], refs[1:]
    cast_out, refs = refs[:len(cast_scales)], refs[len(cast_scales):]
    kbuf, vbuf, bias_ref = refs

    for src, dst, scale in zip(cast_in, cast_out, cast_scales):
        dst[...] = (src[...] if scale == 1 else src[...] * scale).astype(dst.dtype)

    n_seqs, rows = q_ref.shape[:2]
    tk = tq + 2 * halo
    tiles = rows // tq
    tiles_in_seq = seq_len // tq
    kv_tiles = kbuf.shape[2] // LANES
    q_tiles = q_ref.shape[2] // LANES
    step = pl.program_id(2)

    @pl.when((pl.program_id(0) == 0) & (pl.program_id(1) == 0) & (step == 0))
    def _():
        row = lax.broadcasted_iota(jnp.int32, (tq, tk), 0)
        col = lax.broadcasted_iota(jnp.int32, (tq, tk), 1)
        band = jnp.abs(row + halo - col) <= halo
        bias_ref[0] = jnp.where(band & (col >= halo), 0.0, NEG_BIG)
        bias_ref[1] = jnp.where(band, 0.0, NEG_BIG)
        bias_ref[2] = jnp.where(band & (col < tq + halo), 0.0, NEG_BIG)

    kbuf[:, 0:halo, :] = kp_ref[...]
    kbuf[:, halo:halo + rows, :] = kc_ref[...]
    kbuf[:, halo + rows:, :] = kn_ref[...]
    vbuf[:, 0:halo, :] = vp_ref[...]
    vbuf[:, halo:halo + rows, :] = vc_ref[...]
    vbuf[:, halo + rows:, :] = vn_ref[...]

    lane = lax.broadcasted_iota(jnp.int32, (tq, LANES), 1)
    low = lane < HEAD_DIM
    ones = jnp.ones((tk, LANES), jnp.bfloat16)
    zero_q = jnp.zeros((tq, LANES), jnp.bfloat16)

    def scores(n, t, c):
        kc = c if kv_tiles == q_tiles else 0
        r0 = t * tq
        qp = q_ref[n, r0:r0 + tq, c * LANES:(c + 1) * LANES]
        q2 = jnp.concatenate([jnp.where(low, qp, zero_q), jnp.where(low, zero_q, qp)], axis=0)
        kp = kbuf[n, r0:r0 + tk, kc * LANES:(kc + 1) * LANES]
        return lax.dot_general(q2, kp, (((1,), (1,)), ((), ())), preferred_element_type=jnp.float32)

    def finish(n, t, c, s):
        kc = c if kv_tiles == q_tiles else 0
        r0 = t * tq
        g = step * tiles + t
        bias = bias_ref[jnp.where(g == 0, 0, jnp.where(g == tiles_in_seq - 1, 2, 1))]
        vext = jnp.concatenate([vbuf[n, r0:r0 + tk, kc * LANES:(kc + 1) * LANES], ones], axis=1)
        s_lo = s[:tq] + bias
        s_hi = s[tq:] + bias
        m_lo = jnp.max(s_lo, axis=-1, keepdims=True)
        m_hi = jnp.max(s_hi, axis=-1, keepdims=True)
        if sinks:
            sink_lo = sink_ref[A_HEAD_ORDER[2 * c]] * LOG2E
            sink_hi = sink_ref[A_HEAD_ORDER[2 * c + 1]] * LOG2E
            m_lo = jnp.maximum(m_lo, sink_lo)
            m_hi = jnp.maximum(m_hi, sink_hi)
        p = jnp.concatenate([jnp.exp2(s_lo - m_lo), jnp.exp2(s_hi - m_hi)], axis=0).astype(jnp.bfloat16)
        o2 = jnp.dot(p, vext, preferred_element_type=jnp.float32)
        num = jnp.where(low, o2[:tq, :LANES], o2[tq:, :LANES])
        den = jnp.where(low, o2[:tq, LANES:], o2[tq:, LANES:])
        m = jnp.where(low, m_lo, m_hi)
        if sinks:
            den = den + jnp.exp2(jnp.where(low, sink_lo, sink_hi) - m)
        o_ref[n, r0:r0 + tq, c * LANES:(c + 1) * LANES] = (num / den).astype(o_ref.dtype)
        if emit_lse:
            lse_ref[n, r0:r0 + tq, c * LANES:(c + 1) * LANES] = m + jnp.log(den) * LOG2E

    for n, t, c in [(n, t, c) for n in range(n_seqs) for t in range(tiles) for c in range(q_tiles)]:
        finish(n, t, c, scores(n, t, c))


def _const_spec(shape):
    zeros = (0,) * len(shape)
    return pl.BlockSpec(shape, lambda *_: zeros, pipeline_mode=pl.Buffered(1))


def _ffn_in_proj(x3d, pos3d, freq, g1, ffn_w, gm, w_in):
    batch, seq_len, d = x3d.shape
    tm = TOKEN_TILE
    tiles = seq_len // tm
    bf = jnp.bfloat16
    n_tiles = batch * tiles
    ffn_tile = lambda t: jnp.minimum(t, n_tiles - 1)
    proj_tile = lambda t: jnp.maximum(t - 1, 0)
    row = lambda w, which: pl.BlockSpec((None, tm, w), lambda t: (which(t) // tiles, which(t) % tiles, 0))
    dil = lambda dd: pl.BlockSpec((None, dd, tm // dd, B_W),
                                  lambda t: (proj_tile(t) // tiles, 0, proj_tile(t) % tiles, 0))
    nat = lambda w, dt: jax.ShapeDtypeStruct((batch, seq_len, w), dt)
    dil_shape = lambda dd: jax.ShapeDtypeStruct((batch, dd, seq_len // dd, B_W), bf)
    out_specs = [row(d, ffn_tile)] + [row(w, proj_tile) for w in (A_Q_W, A_KV_W, A_KV_W, B_W, B_W, B_W)]
    out_shape = [nat(d, jnp.float32), nat(A_Q_W, bf), nat(A_KV_W, bf), nat(A_KV_W, bf),
                 nat(B_W, bf), nat(B_W, bf), nat(B_W, bf)]
    for _ in range(3):
        out_specs += [dil(dd) for dd in DILATIONS]
        out_shape += [dil_shape(dd) for dd in DILATIONS]
    return pl.pallas_call(
        _ffn_in_proj_kernel,
        grid=(n_tiles + 1,),
        in_specs=[row(d, ffn_tile),
                  pl.BlockSpec((None, tm // ROPE_GROUPS, LANES),
                               lambda t: (proj_tile(t) // tiles, proj_tile(t) % tiles, 0)),
                  _const_spec(freq.shape), _const_spec(g1.shape),
                  *[_const_spec(w.shape) for w in ffn_w], _const_spec(gm.shape), _const_spec(w_in.shape)],
        out_specs=out_specs,
        out_shape=out_shape,
        scratch_shapes=[pltpu.VMEM((tm, d), jnp.bfloat16), pltpu.VMEM((tm, d), jnp.bfloat16),
                        pltpu.VMEM((tm, d), jnp.bfloat16), pltpu.VMEM((tm, d), jnp.float32),
                        pltpu.VMEM((B_W // LANES, tm, LANES), jnp.float32),
                        pltpu.VMEM((DILATIONS[0] * B_W // LANES, tm // DILATIONS[0], LANES), jnp.float32),
                        pltpu.VMEM((tm, LANES), jnp.float32), pltpu.VMEM((tm, LANES), jnp.float32),
                        pltpu.VMEM((d, A_Q_W), jnp.bfloat16)],
        compiler_params=pltpu.CompilerParams(dimension_semantics=("arbitrary",),
                                             vmem_limit_bytes=VMEM_LIMIT),
        name="ffn_in_proj",
    )(x3d, pos3d, freq, g1, *ffn_w, gm, w_in)


def _out_ffn(x1, a, b_parts, wo, g2, ffn_w, gf, final_norm):
    batch, seq_len, d = x1.shape
    tm = TOKEN_TILE
    tiles = seq_len // tm
    row = lambda w: pl.BlockSpec((None, tm, w), lambda b, i: (b, i, 0))
    dil = lambda dd: pl.BlockSpec((None, dd, tm // dd, B_W), lambda b, i: (b, 0, i, 0))
    (o1, l1), (o4, l4), (o16, l16) = b_parts
    d4, d16 = DILATIONS
    slab = lambda: pltpu.VMEM((B_W // LANES, tm, LANES), jnp.float32)
    stage = lambda: pltpu.VMEM((d4 * B_W // LANES, tm // d4, LANES), jnp.float32)
    return pl.pallas_call(
        functools.partial(_out_ffn_kernel, final_norm=final_norm),
        grid=(batch, tiles),
        in_specs=[row(d), row(A_Q_W), row(B_W), row(B_W), dil(d4), dil(d4), dil(d16), dil(d16),
                  _const_spec(wo.shape), _const_spec(g2.shape),
                  *[_const_spec(w.shape) for w in ffn_w], _const_spec(gf.shape)],
        out_specs=row(d),
        out_shape=jax.ShapeDtypeStruct((batch, seq_len, d), jnp.float32),
        scratch_shapes=[pltpu.VMEM((tm, d), jnp.bfloat16), pltpu.VMEM((tm, d), jnp.float32),
                        slab(), slab(), slab(), slab(), stage(), stage()],
        compiler_params=pltpu.CompilerParams(dimension_semantics=("arbitrary", "arbitrary"),
                                             vmem_limit_bytes=VMEM_LIMIT),
        name="out_ffn",
    )(x1, a, o1, l1, o4, l4, o16, l16, wo, g2, *ffn_w, gf)


def _banded_attention(q, k, v, sink, *, tq, halo, step_rows, emit_lse, name, casts=()):
    batch, groups, seq_len, wq = q.shape
    wkv = k.shape[-1]
    rows = min(seq_len, step_rows)
    n_seqs = step_rows // rows
    steps = seq_len // rows
    ratio = rows // halo
    halo_blocks = seq_len // halo
    tile = lambda w: pl.BlockSpec((None, n_seqs, rows, w), lambda b, r, i: (b, r, i, 0))
    prev = pl.BlockSpec((None, n_seqs, halo, wkv), lambda b, r, i: (b, r, jnp.maximum(i * ratio - 1, 0), 0))
    nxt = pl.BlockSpec((None, n_seqs, halo, wkv),
                       lambda b, r, i: (b, r, jnp.minimum((i + 1) * ratio, halo_blocks - 1), 0))
    kv_specs = [prev, tile(wkv), nxt]
    in_specs = [tile(wq), *kv_specs, *kv_specs]
    args = [q, k, k, k, v, v, v]
    if sink is not None:
        in_specs = [pl.BlockSpec(memory_space=pltpu.SMEM)] + in_specs
        args = [sink] + args
    out_specs = [tile(wq)]
    out_shape = [jax.ShapeDtypeStruct(q.shape, jnp.bfloat16)]
    if emit_lse:
        out_specs.append(tile(wq))
        out_shape.append(jax.ShapeDtypeStruct(q.shape, jnp.float32))
    grid = (batch, groups // n_seqs, steps)
    n_steps = grid[0] * grid[1] * grid[2]
    flat_step = lambda b, r, i: (b * grid[1] + r) * grid[2] + i
    for w, _, src_block in casts:
        block = (w.shape[0] // n_steps, w.shape[1])
        in_specs.append(pl.BlockSpec(block, lambda b, r, i, f=src_block: (f(flat_step(b, r, i)), 0)))
        args.append(w)
        out_specs.append(pl.BlockSpec(block, lambda b, r, i: (flat_step(b, r, i), 0)))
        out_shape.append(jax.ShapeDtypeStruct(w.shape, jnp.bfloat16))
    tk = tq + 2 * halo
    res = pl.pallas_call(
        functools.partial(_attn_kernel, seq_len=seq_len, tq=tq, halo=halo, sinks=sink is not None,
                          emit_lse=emit_lse, cast_scales=tuple(scale for _, scale, _ in casts)),
        grid=grid,
        in_specs=in_specs,
        out_specs=out_specs,
        out_shape=out_shape,
        scratch_shapes=[pltpu.VMEM((n_seqs, rows + 2 * halo, wkv), jnp.bfloat16),
                        pltpu.VMEM((n_seqs, rows + 2 * halo, wkv), jnp.bfloat16),
                        pltpu.VMEM((3, tq, tk), jnp.float32)],
        compiler_params=pltpu.CompilerParams(dimension_semantics=("arbitrary",) * 3,
                                             vmem_limit_bytes=VMEM_LIMIT),
        name=name,
    )(*args)
    return res


def kernel(x, positions, norm_ffn1, w_gate1, w_up1, w_down1, norm_mix, w_in, a_sink, w_out,
           norm_ffn2, w_gate2, w_up2, w_down2, norm_final):
    batch, seq_len, d = x.shape
    depth = norm_ffn1.shape[0]
    bf = jnp.bfloat16

    inv_freq = 1.0 / (ROPE_THETA ** (jnp.arange(0, HEAD_DIM, 2, dtype=jnp.float32) / HEAD_DIM))
    freq = jnp.tile(inv_freq, LANES // (HEAD_DIM // 2))[None, :]
    pos3d = jnp.repeat(positions.reshape(batch, seq_len // ROPE_GROUPS, ROPE_GROUPS), HEAD_DIM // 2, axis=-1)
    xc = x
    same_block = lambda s: s
    per_group = A_Q_HEADS // A_KV_HEADS
    w_out_block = lambda s: jnp.where(s < A_Q_HEADS, (s % A_KV_HEADS) * per_group + s // A_KV_HEADS, s)

    for l in range(depth):
        ffn1 = (w_gate1[l].astype(bf), w_up1[l].astype(bf), (w_down1[l] * FFN_RES_WEIGHT).astype(bf))

        outs = _ffn_in_proj(xc, pos3d, freq, norm_ffn1[l][None, :], ffn1, norm_mix[l][None, :], w_in[l].astype(bf))
        x1, aq, ak, av, bq, bk, bv = outs[:7]
        n_dil = len(DILATIONS)
        bq_d, bk_d, bv_d = outs[7:7 + n_dil], outs[7 + n_dil:7 + 2 * n_dil], outs[7 + 2 * n_dil:]

        a_out, *ffn2, wo = _banded_attention(
            aq[:, None], ak[:, None], av[:, None], a_sink[l],
            tq=2 * A_HALF_WINDOW, halo=A_HALF_WINDOW, step_rows=ATTN_STEP_ROWS, emit_lse=False, name="attn_a",
            casts=((w_gate2[l], 1, same_block), (w_up2[l], 1, same_block),
                   (w_down2[l], FFN_RES_WEIGHT, same_block), (w_out[l], 1, w_out_block)))
        assert wo.shape[0] // (batch * (seq_len // ATTN_STEP_ROWS)) == HEAD_DIM
        b_parts = []
        for window, dilation in B_PATTERNS:
            if dilation == 1:
                qkv = (bq[:, None], bk[:, None], bv[:, None])
            else:
                j = DILATIONS.index(dilation)
                qkv = (bq_d[j], bk_d[j], bv_d[j])
            halo = window // (2 * dilation)
            o, lse = _banded_attention(*qkv, None, tq=2 * halo, halo=halo, step_rows=ATTN_STEP_ROWS,
                                       emit_lse=True, name=f"attn_b_d{dilation}")
            if dilation == 1:
                o, lse = o[:, 0], lse[:, 0]
            b_parts.append((o, lse))

        xc = _out_ffn(x1, a_out[:, 0], b_parts, wo, norm_ffn2[l][None, :], ffn2,
                      norm_final[None, :], final_norm=l == depth - 1)
    return xc
```

```python
import functools

import jax
import jax.numpy as jnp
from jax import lax
from jax.experimental import pallas as pl
from jax.experimental.pallas import tpu as pltpu

HEAD_DIM = 64
A_Q_HEADS = 8
A_KV_HEADS = 2
A_HALF_WINDOW = 128
B_HEADS = 8
B_PATTERNS = ((128, 1), (512, 4), (2048, 16))
ROPE_THETA = 10000.0
NORM_EPS = 1e-6
FFN_RES_WEIGHT = 0.5

A_Q_W = A_Q_HEADS * HEAD_DIM
A_KV_W = A_KV_HEADS * HEAD_DIM
B_W = B_HEADS * HEAD_DIM

LANES = 128
MXU_DEPTH = 256
NEG_BIG = -1e30
LOG2E = 1.4426950408889634
TOKEN_TILE = 512
FF_CHUNK = 256
ATTN_STEP_ROWS = 2048
VMEM_LIMIT = 56 * 1024 * 1024
A_HEAD_ORDER = (0, 4, 1, 5, 2, 6, 3, 7)
DILATIONS = tuple(d for _, d in B_PATTERNS if d > 1)
ROPE_GROUPS = LANES // (HEAD_DIM // 2)


def _rms_norm(x, g):
    y = x * lax.rsqrt(jnp.mean(x * x, axis=-1, keepdims=True) + NORM_EPS)
    return y * g


def _swiglu_accumulate(h_ref, wg_ref, wu_ref, wd_ref, acc_ref, before_chunk=None):
    d_ff = wg_ref.shape[1]
    for c in range(d_ff // FF_CHUNK):
        if before_chunk and c in before_chunk:
            before_chunk[c]()
        sl = slice(c * FF_CHUNK, (c + 1) * FF_CHUNK)
        g = jnp.dot(h_ref[...], wg_ref[:, sl], preferred_element_type=jnp.float32)
        u = jnp.dot(h_ref[...], wu_ref[:, sl], preferred_element_type=jnp.float32)
        a = (g * jax.nn.sigmoid(g) * u).astype(jnp.bfloat16)
        down = jnp.dot(a, wd_ref[sl, :], preferred_element_type=jnp.float32)
        if c == 0:
            acc_ref[...] = down
        else:
            acc_ref[...] += down


def _rope(t, cos, sin_signed, first_half):
    parts = []
    for j in range(t.shape[1] // LANES):
        tj = t[:, j * LANES:(j + 1) * LANES]
        partner = jnp.where(first_half, pltpu.roll(tj, LANES - HEAD_DIM // 2, 1), pltpu.roll(tj, HEAD_DIM // 2, 1))
        parts.append(tj * cos + partner * sin_signed)
    return parts[0] if len(parts) == 1 else jnp.concatenate(parts, axis=1)


def _write_deinterleaved(val, slab_ref, stage_ref, out_refs):
    tm, width = val.shape
    n_slabs = width // LANES
    d4, d16 = DILATIONS
    ratio = d16 // d4
    rows4 = tm // d4
    for c in range(n_slabs):
        slab_ref[c] = val[:, c * LANES:(c + 1) * LANES]
    for r in range(d4):
        for c in range(n_slabs):
            part = slab_ref[c, pl.ds(r, rows4, stride=d4), :]
            out_refs[d4][r, :, c * LANES:(c + 1) * LANES] = part.astype(out_refs[d4].dtype)
            stage_ref[r * n_slabs + c] = part
    for r in range(d4):
        for q in range(ratio):
            for c in range(n_slabs):
                part = stage_ref[r * n_slabs + c, pl.ds(q, rows4 // ratio, stride=ratio), :]
                out_refs[d16][q * d4 + r, :, c * LANES:(c + 1) * LANES] = part.astype(out_refs[d16].dtype)


def _ffn_in_proj_kernel(x_ref, pos_ref, freq_ref, g1_ref, wg_ref, wu_ref, wd_ref, gm_ref, win_ref,
                        x1_ref, aq_ref, ak_ref, av_ref, bq_ref, bk_ref, bv_ref, *rest):
    n_dil = len(DILATIONS)
    bq_d = dict(zip(DILATIONS, rest[0:n_dil]))
    bk_d = dict(zip(DILATIONS, rest[n_dil:2 * n_dil]))
    bv_d = dict(zip(DILATIONS, rest[2 * n_dil:3 * n_dil]))
    h_ref, h2_cur_ref, h2_next_ref, acc_ref, slab_ref, stage_ref, cos_ref, sin_ref, waq_ref = rest[3 * n_dil:]

    @pl.when(pl.program_id(0) == 0)
    def _():
        h2_next_ref[...] = jnp.zeros_like(h2_next_ref)
        in_low = lax.broadcasted_iota(jnp.int32, (win_ref.shape[0], LANES), 1) < HEAD_DIM
        per_group = A_Q_HEADS // A_KV_HEADS
        for c in range(A_Q_W // LANES):
            halves = []
            for dst_half, head in enumerate((c, c + per_group)):
                lo = (head // 2) * LANES
                src = win_ref[:, lo:lo + LANES].astype(jnp.float32)
                halves.append(src if head % 2 == dst_half else pltpu.roll(src, HEAD_DIM, 1))
            waq_ref[:, c * LANES:(c + 1) * LANES] = jnp.where(in_low, halves[0], halves[1]).astype(waq_ref.dtype)

    h2_cur_ref[...] = h2_next_ref[...]

    half = HEAD_DIM // 2
    tm = cos_ref.shape[0]
    ang = pos_ref[...].astype(jnp.float32) * freq_ref[...]
    lane_d = lax.broadcasted_iota(jnp.int32, ang.shape, 1)
    first_half_d = (lane_d % HEAD_DIM) < half
    tables = ((jnp.cos(ang), cos_ref, False), (jnp.sin(ang), sin_ref, True))
    for k in range(ROPE_GROUPS):
        in_group = (lane_d // half) == k
        for table, ref, signed in tables:
            one = jnp.where(in_group, table, 0.0)
            full = one + pltpu.roll(one, half, 1)
            full = full + pltpu.roll(full, 2 * half, 1)
            if signed:
                full = jnp.where(first_half_d, -full, full)
            ref[pl.ds(k, tm // ROPE_GROUPS, stride=ROPE_GROUPS), :] = full
    cos = cos_ref[...]
    sin_signed = sin_ref[...]
    lane = lax.broadcasted_iota(jnp.int32, cos.shape, 1)
    first_half = (lane % HEAD_DIM) < half
    scale = HEAD_DIM ** -0.5 * LOG2E

    def proj(src_ref, lo, width):
        return jnp.dot(src_ref[...], win_ref[:, lo:lo + width], preferred_element_type=jnp.float32)

    offs = {}
    o = 0
    for name, width in (("aq", A_Q_W), ("ak", A_KV_W), ("av", A_KV_W), ("bq", B_W), ("bk", B_W), ("bv", B_W)):
        offs[name] = (o, width)
        o += width
    rope = lambda t: _rope(t, cos, sin_signed, first_half)

    def emit_bv():
        bv = proj(h2_next_ref, *offs["bv"])
        bv_ref[...] = bv.astype(bv_ref.dtype)
        _write_deinterleaved(bv, slab_ref, stage_ref, bv_d)

    def emit_bk():
        bk = rope(proj(h2_next_ref, *offs["bk"]))
        bk_ref[...] = bk.astype(bk_ref.dtype)
        _write_deinterleaved(bk, slab_ref, stage_ref, bk_d)

    def emit_bq():
        bq = rope(proj(h2_next_ref, *offs["bq"])) * scale
        bq_ref[...] = bq.astype(bq_ref.dtype)
        _write_deinterleaved(bq, slab_ref, stage_ref, bq_d)

    x = x_ref[...]
    h_ref[...] = _rms_norm(x, g1_ref[...]).astype(jnp.bfloat16)
    _swiglu_accumulate(h_ref, wg_ref, wu_ref, wd_ref, acc_ref, before_chunk={0: emit_bv, 3: emit_bk, 6: emit_bq})
    x1 = x + acc_ref[...]
    x1_ref[...] = x1
    h2_next_ref[...] = _rms_norm(x1, gm_ref[...]).astype(jnp.bfloat16)

    aq = jnp.dot(h2_cur_ref[...], waq_ref[...], preferred_element_type=jnp.float32)
    aq_ref[...] = (rope(aq) * scale).astype(aq_ref.dtype)
    ak_ref[...] = rope(proj(h2_cur_ref, *offs["ak"])).astype(ak_ref.dtype)
    av_ref[...] = proj(h2_cur_ref, *offs["av"]).astype(av_ref.dtype)


def _interleave_into(slab_ref, blk_ref, d, stage_ref=None):
    n_slabs = slab_ref.shape[0]
    rows = blk_ref.shape[1]
    if stage_ref is None:
        for r in range(d):
            for c in range(n_slabs):
                slab_ref[c, pl.ds(r, rows, stride=d), :] = blk_ref[r, :, c * LANES:(c + 1) * LANES].astype(jnp.float32)
        return
    d4 = DILATIONS[0]
    ratio = d // d4
    for r in range(d4):
        for c in range(n_slabs):
            for q in range(ratio):
                part = blk_ref[q * d4 + r, :, c * LANES:(c + 1) * LANES].astype(jnp.float32)
                stage_ref[r * n_slabs + c, pl.ds(q, rows, stride=ratio), :] = part
            slab_ref[c, pl.ds(r, rows * ratio, stride=d4), :] = stage_ref[r * n_slabs + c]


def _out_ffn_kernel(x1_ref, a_ref, o1_ref, l1_ref, o4_ref, l4_ref, o16_ref, l16_ref,
                    wo_ref, g2_ref, wg_ref, wu_ref, wd_ref, gf_ref,
                    out_ref, h_ref, acc_ref, so4_ref, sl4_ref, so16_ref, sl16_ref, stage_o_ref, stage_l_ref,
                    *, final_norm):
    d4, d16 = DILATIONS
    _interleave_into(so4_ref, o4_ref, d4)
    _interleave_into(sl4_ref, l4_ref, d4)
    _interleave_into(so16_ref, o16_ref, d16, stage_o_ref)
    _interleave_into(sl16_ref, l16_ref, d16, stage_l_ref)

    def merged(c):
        sl = slice(c * LANES, (c + 1) * LANES)
        la, lb, lc = l1_ref[:, sl], sl4_ref[c], sl16_ref[c]
        m = jnp.maximum(jnp.maximum(la, lb), lc)
        wa, wb, wc = jnp.exp2(la - m), jnp.exp2(lb - m), jnp.exp2(lc - m)
        num = wa * o1_ref[:, sl].astype(jnp.float32) + wb * so4_ref[c] + wc * so16_ref[c]
        return (num / (wa + wb + wc)).astype(jnp.bfloat16)

    mix = jnp.dot(a_ref[...], wo_ref[:A_Q_W, :], preferred_element_type=jnp.float32)
    per_dot = MXU_DEPTH // LANES
    for c in range(0, B_W // LANES, per_dot):
        b = jnp.concatenate([merged(c + j) for j in range(per_dot)], axis=1)
        mix += jnp.dot(b, wo_ref[A_Q_W + c * LANES:A_Q_W + (c + per_dot) * LANES, :],
                       preferred_element_type=jnp.float32)
    x2 = x1_ref[...] + mix
    h_ref[...] = _rms_norm(x2, g2_ref[...]).astype(jnp.bfloat16)
    _swiglu_accumulate(h_ref, wg_ref, wu_ref, wd_ref, acc_ref)
    x3 = x2 + acc_ref[...]
    out_ref[...] = _rms_norm(x3, gf_ref[...]) if final_norm else x3


def _attn_kernel(*refs, seq_len, tq, halo, sinks, emit_lse, cast_scales):
    if sinks:
        sink_ref, refs = refs[0], refs[1:]
    q_ref, kp_ref, kc_ref, kn_ref, vp_ref, vc_ref, vn_ref = refs[:7]
    refs = refs[7:]
    cast_in, refs = refs[:len(cast_scales)], refs[len(cast_scales):]
    o_ref, refs = refs[0], refs[1:]
    if emit_lse:
        lse_ref, refs = refs[0], refs[1:]
    cast_out, refs = refs[:len(cast_scales)], refs[len(cast_scales):]
    kbuf, vbuf, bias_ref = refs

    for src, dst, scale in zip(cast_in, cast_out, cast_scales):
        dst[...] = (src[...] if scale == 1 else src[...] * scale).astype(dst.dtype)

    n_seqs, rows = q_ref.shape[:2]
    tk = tq + 2 * halo
    tiles = rows // tq
    tiles_in_seq = seq_len // tq
    kv_tiles = kbuf.shape[2] // LANES
    q_tiles = q_ref.shape[2] // LANES
    step = pl.program_id(2)

    @pl.when((pl.program_id(0) == 0) & (pl.program_id(1) == 0) & (step == 0))
    def _():
        row = lax.broadcasted_iota(jnp.int32, (tq, tk), 0)
        col = lax.broadcasted_iota(jnp.int32, (tq, tk), 1)
        band = jnp.abs(row + halo - col) <= halo
        bias_ref[0] = jnp.where(band & (col >= halo), 0.0, NEG_BIG)
        bias_ref[1] = jnp.where(band, 0.0, NEG_BIG)
        bias_ref[2] = jnp.where(band & (col < tq + halo), 0.0, NEG_BIG)

    kbuf[:, 0:halo, :] = kp_ref[...]
    kbuf[:, halo:halo + rows, :] = kc_ref[...]
    kbuf[:, halo + rows:, :] = kn_ref[...]
    vbuf[:, 0:halo, :] = vp_ref[...]
    vbuf[:, halo:halo + rows, :] = vc_ref[...]
    vbuf[:, halo + rows:, :] = vn_ref[...]

    lane = lax.broadcasted_iota(jnp.int32, (tq, LANES), 1)
    low = lane < HEAD_DIM
    ones = jnp.ones((tk, LANES), jnp.bfloat16)
    zero_q = jnp.zeros((tq, LANES), jnp.bfloat16)

    def scores(n, t, c):
        kc = c if kv_tiles == q_tiles else 0
        r0 = t * tq
        qp = q_ref[n, r0:r0 + tq, c * LANES:(c + 1) * LANES]
        q2 = jnp.concatenate([jnp.where(low, qp, zero_q), jnp.where(low, zero_q, qp)], axis=0)
        kp = kbuf[n, r0:r0 + tk, kc * LANES:(kc + 1) * LANES]
        return lax.dot_general(q2, kp, (((1,), (1,)), ((), ())), preferred_element_type=jnp.float32)

    def finish(n, t, c, s):
        kc = c if kv_tiles == q_tiles else 0
        r0 = t * tq
        g = step * tiles + t
        bias = bias_ref[jnp.where(g == 0, 0, jnp.where(g == tiles_in_seq - 1, 2, 1))]
        vext = jnp.concatenate([vbuf[n, r0:r0 + tk, kc * LANES:(kc + 1) * LANES], ones], axis=1)
        s_lo = s[:tq] + bias
        s_hi = s[tq:] + bias
        m_lo = jnp.max(s_lo, axis=-1, keepdims=True)
        m_hi = jnp.max(s_hi, axis=-1, keepdims=True)
        if sinks:
            sink_lo = sink_ref[A_HEAD_ORDER[2 * c]] * LOG2E
            sink_hi = sink_ref[A_HEAD_ORDER[2 * c + 1]] * LOG2E
            m_lo = jnp.maximum(m_lo, sink_lo)
            m_hi = jnp.maximum(m_hi, sink_hi)
        p = jnp.concatenate([jnp.exp2(s_lo - m_lo), jnp.exp2(s_hi - m_hi)], axis=0).astype(jnp.bfloat16)
        o2 = jnp.dot(p, vext, preferred_element_type=jnp.float32)
        num = jnp.where(low, o2[:tq, :LANES], o2[tq:, :LANES])
        den = jnp.where(low, o2[:tq, LANES:], o2[tq:, LANES:])
        m = jnp.where(low, m_lo, m_hi)
        if sinks:
            den = den + jnp.exp2(jnp.where(low, sink_lo, sink_hi) - m)
        o_ref[n, r0:r0 + tq, c * LANES:(c + 1) * LANES] = (num / den).astype(o_ref.dtype)
        if emit_lse:
            lse_ref[n, r0:r0 + tq, c * LANES:(c + 1) * LANES] = m + jnp.log(den) * LOG2E

    for n, t, c in [(n, t, c) for n in range(n_seqs) for t in range(tiles) for c in range(q_tiles)]:
        finish(n, t, c, scores(n, t, c))


def _const_spec(shape):
    zeros = (0,) * len(shape)
    return pl.BlockSpec(shape, lambda *_: zeros, pipeline_mode=pl.Buffered(1))


def _ffn_in_proj(x3d, pos3d, freq, g1, ffn_w, gm, w_in):
    batch, seq_len, d = x3d.shape
    tm = TOKEN_TILE
    tiles = seq_len // tm
    bf = jnp.bfloat16
    n_tiles = batch * tiles
    ffn_tile = lambda t: jnp.minimum(t, n_tiles - 1)
    proj_tile = lambda t: jnp.maximum(t - 1, 0)
    row = lambda w, which: pl.BlockSpec((None, tm, w), lambda t: (which(t) // tiles, which(t) % tiles, 0))
    dil = lambda dd: pl.BlockSpec((None, dd, tm // dd, B_W),
                                  lambda t: (proj_tile(t) // tiles, 0, proj_tile(t) % tiles, 0))
    nat = lambda w, dt: jax.ShapeDtypeStruct((batch, seq_len, w), dt)
    dil_shape = lambda dd: jax.ShapeDtypeStruct((batch, dd, seq_len // dd, B_W), bf)
    out_specs = [row(d, ffn_tile)] + [row(w, proj_tile) for w in (A_Q_W, A_KV_W, A_KV_W, B_W, B_W, B_W)]
    out_shape = [nat(d, jnp.float32), nat(A_Q_W, bf), nat(A_KV_W, bf), nat(A_KV_W, bf),
                 nat(B_W, bf), nat(B_W, bf), nat(B_W, bf)]
    for _ in range(3):
        out_specs += [dil(dd) for dd in DILATIONS]
        out_shape += [dil_shape(dd) for dd in DILATIONS]
    return pl.pallas_call(
        _ffn_in_proj_kernel,
        grid=(n_tiles + 1,),
        in_specs=[row(d, ffn_tile),
                  pl.BlockSpec((None, tm // ROPE_GROUPS, LANES),
                               lambda t: (proj_tile(t) // tiles, proj_tile(t) % tiles, 0)),
                  _const_spec(freq.shape), _const_spec(g1.shape),
                  *[_const_spec(w.shape) for w in ffn_w], _const_spec(gm.shape), _const_spec(w_in.shape)],
        out_specs=out_specs,
        out_shape=out_shape,
        scratch_shapes=[pltpu.VMEM((tm, d), jnp.bfloat16), pltpu.VMEM((tm, d), jnp.bfloat16),
                        pltpu.VMEM((tm, d), jnp.bfloat16), pltpu.VMEM((tm, d), jnp.float32),
                        pltpu.VMEM((B_W // LANES, tm, LANES), jnp.float32),
                        pltpu.VMEM((DILATIONS[0] * B_W // LANES, tm // DILATIONS[0], LANES), jnp.float32),
                        pltpu.VMEM((tm, LANES), jnp.float32), pltpu.VMEM((tm, LANES), jnp.float32),
                        pltpu.VMEM((d, A_Q_W), jnp.bfloat16)],
        compiler_params=pltpu.CompilerParams(dimension_semantics=("arbitrary",),
                                             vmem_limit_bytes=VMEM_LIMIT),
        name="ffn_in_proj",
    )(x3d, pos3d, freq, g1, *ffn_w, gm, w_in)


def _out_ffn(x1, a, b_parts, wo, g2, ffn_w, gf, final_norm):
    batch, seq_len, d = x1.shape
    tm = TOKEN_TILE
    tiles = seq_len // tm
    row = lambda w: pl.BlockSpec((None, tm, w), lambda b, i: (b, i, 0))
    dil = lambda dd: pl.BlockSpec((None, dd, tm // dd, B_W), lambda b, i: (b, 0, i, 0))
    (o1, l1), (o4, l4), (o16, l16) = b_parts
    d4, d16 = DILATIONS
    slab = lambda: pltpu.VMEM((B_W // LANES, tm, LANES), jnp.float32)
    stage = lambda: pltpu.VMEM((d4 * B_W // LANES, tm // d4, LANES), jnp.float32)
    return pl.pallas_call(
        functools.partial(_out_ffn_kernel, final_norm=final_norm),
        grid=(batch, tiles),
        in_specs=[row(d), row(A_Q_W), row(B_W), row(B_W), dil(d4), dil(d4), dil(d16), dil(d16),
                  _const_spec(wo.shape), _const_spec(g2.shape),
                  *[_const_spec(w.shape) for w in ffn_w], _const_spec(gf.shape)],
        out_specs=row(d),
        out_shape=jax.ShapeDtypeStruct((batch, seq_len, d), jnp.float32),
        scratch_shapes=[pltpu.VMEM((tm, d), jnp.bfloat16), pltpu.VMEM((tm, d), jnp.float32),
                        slab(), slab(), slab(), slab(), stage(), stage()],
        compiler_params=pltpu.CompilerParams(dimension_semantics=("arbitrary", "arbitrary"),
                                             vmem_limit_bytes=VMEM_LIMIT),
        name="out_ffn",
    )(x1, a, o1, l1, o4, l4, o16, l16, wo, g2, *ffn_w, gf)


def _banded_attention(q, k, v, sink, *, tq, halo, step_rows, emit_lse, name, casts=()):
    batch, groups, seq_len, wq = q.shape
    wkv = k.shape[-1]
    rows = min(seq_len, step_rows)
    n_seqs = step_rows // rows
    steps = seq_len // rows
    ratio = rows // halo
    halo_blocks = seq_len // halo
    tile = lambda w: pl.BlockSpec((None, n_seqs, rows, w), lambda b, r, i: (b, r, i, 0))
    prev = pl.BlockSpec((None, n_seqs, halo, wkv), lambda b, r, i: (b, r, jnp.maximum(i * ratio - 1, 0), 0))
    nxt = pl.BlockSpec((None, n_seqs, halo, wkv),
                       lambda b, r, i: (b, r, jnp.minimum((i + 1) * ratio, halo_blocks - 1), 0))
    kv_specs = [prev, tile(wkv), nxt]
    in_specs = [tile(wq), *kv_specs, *kv_specs]
    args = [q, k, k, k, v, v, v]
    if sink is not None:
        in_specs = [pl.BlockSpec(memory_space=pltpu.SMEM)] + in_specs
        args = [sink] + args
    out_specs = [tile(wq)]
    out_shape = [jax.ShapeDtypeStruct(q.shape, jnp.bfloat16)]
    if emit_lse:
        out_specs.append(tile(wq))
        out_shape.append(jax.ShapeDtypeStruct(q.shape, jnp.float32))
    grid = (batch, groups // n_seqs, steps)
    n_steps = grid[0] * grid[1] * grid[2]
    flat_step = lambda b, r, i: (b * grid[1] + r) * grid[2] + i
    for w, _, src_block in casts:
        block = (w.shape[0] // n_steps, w.shape[1])
        in_specs.append(pl.BlockSpec(block, lambda b, r, i, f=src_block: (f(flat_step(b, r, i)), 0)))
        args.append(w)
        out_specs.append(pl.BlockSpec(block, lambda b, r, i: (flat_step(b, r, i), 0)))
        out_shape.append(jax.ShapeDtypeStruct(w.shape, jnp.bfloat16))
    tk = tq + 2 * halo
    res = pl.pallas_call(
        functools.partial(_attn_kernel, seq_len=seq_len, tq=tq, halo=halo, sinks=sink is not None,
                          emit_lse=emit_lse, cast_scales=tuple(scale for _, scale, _ in casts)),
        grid=grid,
        in_specs=in_specs,
        out_specs=out_specs,
        out_shape=out_shape,
        scratch_shapes=[pltpu.VMEM((n_seqs, rows + 2 * halo, wkv), jnp.bfloat16),
                        pltpu.VMEM((n_seqs, rows + 2 * halo, wkv), jnp.bfloat16),
                        pltpu.VMEM((3, tq, tk), jnp.float32)],
        compiler_params=pltpu.CompilerParams(dimension_semantics=("arbitrary",) * 3,
                                             vmem_limit_bytes=VMEM_LIMIT),
        name=name,
    )(*args)
    return res


def kernel(x, positions, norm_ffn1, w_gate1, w_up1, w_down1, norm_mix, w_in, a_sink, w_out,
           norm_ffn2, w_gate2, w_up2, w_down2, norm_final):
    batch, seq_len, d = x.shape
    depth = norm_ffn1.shape[0]
    bf = jnp.bfloat16

    inv_freq = 1.0 / (ROPE_THETA ** (jnp.arange(0, HEAD_DIM, 2, dtype=jnp.float32) / HEAD_DIM))
    freq = jnp.tile(inv_freq, LANES // (HEAD_DIM // 2))[None, :]
    pos3d = jnp.repeat(positions.reshape(batch, seq_len // ROPE_GROUPS, ROPE_GROUPS), HEAD_DIM // 2, axis=-1)
    xc = x
    same_block = lambda s: s
    per_group = A_Q_HEADS // A_KV_HEADS
    w_out_block = lambda s: jnp.where(s < A_Q_HEADS, (s % A_KV_HEADS) * per_group + s // A_KV_HEADS, s)

    for l in range(depth):
        ffn1 = (w_gate1[l].astype(bf), w_up1[l].astype(bf), (w_down1[l] * FFN_RES_WEIGHT).astype(bf))

        outs = _ffn_in_proj(xc, pos3d, freq, norm_ffn1[l][None, :], ffn1, norm_mix[l][None, :], w_in[l].astype(bf))
        x1, aq, ak, av, bq, bk, bv = outs[:7]
        n_dil = len(DILATIONS)
        bq_d, bk_d, bv_d = outs[7:7 + n_dil], outs[7 + n_dil:7 + 2 * n_dil], outs[7 + 2 * n_dil:]

        a_out, *ffn2, wo = _banded_attention(
            aq[:, None], ak[:, None], av[:, None], a_sink[l],
            tq=2 * A_HALF_WINDOW, halo=A_HALF_WINDOW, step_rows=ATTN_STEP_ROWS, emit_lse=False, name="attn_a",
            casts=((w_gate2[l], 1, same_block), (w_up2[l], 1, same_block),
                   (w_down2[l], FFN_RES_WEIGHT, same_block), (w_out[l], 1, w_out_block)))
        assert wo.shape[0] // (batch * (seq_len // ATTN_STEP_ROWS)) == HEAD_DIM
        b_parts = []
        for window, dilation in B_PATTERNS:
            if dilation == 1:
                qkv = (bq[:, None], bk[:, None], bv[:, None])
            else:
                j = DILATIONS.index(dilation)
                qkv = (bq_d[j], bk_d[j], bv_d[j])
            halo = window // (2 * dilation)
            o, lse = _banded_attention(*qkv, None, tq=2 * halo, halo=halo, step_rows=ATTN_STEP_ROWS,
                                       emit_lse=True, name=f"attn_b_d{dilation}")
            if dilation == 1:
                o, lse = o[:, 0], lse[:, 0]
            b_parts.append((o, lse))

        xc = _out_ffn(x1, a_out[:, 0], b_parts, wo, norm_ffn2[l][None, :], ffn2,
                      norm_final[None, :], final_norm=l == depth - 1)
    return xc
```

```python
import functools

import jax
import jax.numpy as jnp
from jax import lax
from jax.experimental import pallas as pl
from jax.experimental.pallas import tpu as pltpu

HEAD_DIM = 64
A_Q_HEADS = 8
A_KV_HEADS = 2
A_HALF_WINDOW = 128
B_HEADS = 8
B_PATTERNS = ((128, 1), (512, 4), (2048, 16))
ROPE_THETA = 10000.0
NORM_EPS = 1e-6
FFN_RES_WEIGHT = 0.5

A_Q_W = A_Q_HEADS * HEAD_DIM
A_KV_W = A_KV_HEADS * HEAD_DIM
B_W = B_HEADS * HEAD_DIM

LANES = 128
MXU_DEPTH = 256
NEG_BIG = -1e30
LOG2E = 1.4426950408889634
TOKEN_TILE = 512
FF_CHUNK = 256
ATTN_STEP_ROWS = 2048
VMEM_LIMIT = 56 * 1024 * 1024
A_HEAD_ORDER = (0, 4, 1, 5, 2, 6, 3, 7)
DILATIONS = tuple(d for _, d in B_PATTERNS if d > 1)
ROPE_GROUPS = LANES // (HEAD_DIM // 2)


def _rms_norm(x, g):
    y = x * lax.rsqrt(jnp.mean(x * x, axis=-1, keepdims=True) + NORM_EPS)
    return y * g


def _swiglu_accumulate(h_ref, wg_ref, wu_ref, wd_ref, acc_ref, before_chunk=None):
    d_ff = wg_ref.shape[1]
    for c in range(d_ff // FF_CHUNK):
        if before_chunk and c in before_chunk:
            before_chunk[c]()
        sl = slice(c * FF_CHUNK, (c + 1) * FF_CHUNK)
        g = jnp.dot(h_ref[...], wg_ref[:, sl], preferred_element_type=jnp.float32)
        u = jnp.dot(h_ref[...], wu_ref[:, sl], preferred_element_type=jnp.float32)
        a = (g * jax.nn.sigmoid(g) * u).astype(jnp.bfloat16)
        down = jnp.dot(a, wd_ref[sl, :], preferred_element_type=jnp.float32)
        if c == 0:
            acc_ref[...] = down
        else:
            acc_ref[...] += down


def _rope(t, cos, sin_signed, first_half):
    parts = []
    for j in range(t.shape[1] // LANES):
        tj = t[:, j * LANES:(j + 1) * LANES]
        partner = jnp.where(first_half, pltpu.roll(tj, LANES - HEAD_DIM // 2, 1), pltpu.roll(tj, HEAD_DIM // 2, 1))
        parts.append(tj * cos + partner * sin_signed)
    return parts[0] if len(parts) == 1 else jnp.concatenate(parts, axis=1)


def _write_deinterleaved(val, slab_ref, stage_ref, out_refs):
    tm, width = val.shape
    n_slabs = width // LANES
    d4, d16 = DILATIONS
    ratio = d16 // d4
    rows4 = tm // d4
    for c in range(n_slabs):
        slab_ref[c] = val[:, c * LANES:(c + 1) * LANES]
    for r in range(d4):
        for c in range(n_slabs):
            part = slab_ref[c, pl.ds(r, rows4, stride=d4), :]
            out_refs[d4][r, :, c * LANES:(c + 1) * LANES] = part.astype(out_refs[d4].dtype)
            stage_ref[r * n_slabs + c] = part
    for r in range(d4):
        for q in range(ratio):
            for c in range(n_slabs):
                part = stage_ref[r * n_slabs + c, pl.ds(q, rows4 // ratio, stride=ratio), :]
                out_refs[d16][q * d4 + r, :, c * LANES:(c + 1) * LANES] = part.astype(out_refs[d16].dtype)


def _ffn_in_proj_kernel(x_ref, pos_ref, freq_ref, g1_ref, wg_ref, wu_ref, wd_ref, gm_ref, win_ref,
                        x1_ref, aq_ref, ak_ref, av_ref, bq_ref, bk_ref, bv_ref, *rest):
    n_dil = len(DILATIONS)
    bq_d = dict(zip(DILATIONS, rest[0:n_dil]))
    bk_d = dict(zip(DILATIONS, rest[n_dil:2 * n_dil]))
    bv_d = dict(zip(DILATIONS, rest[2 * n_dil:3 * n_dil]))
    h_ref, h2_cur_ref, h2_next_ref, acc_ref, slab_ref, stage_ref, cos_ref, sin_ref, waq_ref = rest[3 * n_dil:]

    @pl.when(pl.program_id(0) == 0)
    def _():
        h2_next_ref[...] = jnp.zeros_like(h2_next_ref)
        in_low = lax.broadcasted_iota(jnp.int32, (win_ref.shape[0], LANES), 1) < HEAD_DIM
        per_group = A_Q_HEADS // A_KV_HEADS
        for c in range(A_Q_W // LANES):
            halves = []
            for dst_half, head in enumerate((c, c + per_group)):
                lo = (head // 2) * LANES
                src = win_ref[:, lo:lo + LANES].astype(jnp.float32)
                halves.append(src if head % 2 == dst_half else pltpu.roll(src, HEAD_DIM, 1))
            waq_ref[:, c * LANES:(c + 1) * LANES] = jnp.where(in_low, halves[0], halves[1]).astype(waq_ref.dtype)

    h2_cur_ref[...] = h2_next_ref[...]

    half = HEAD_DIM // 2
    tm = cos_ref.shape[0]
    ang = pos_ref[...].astype(jnp.float32) * freq_ref[...]
    lane_d = lax.broadcasted_iota(jnp.int32, ang.shape, 1)
    first_half_d = (lane_d % HEAD_DIM) < half
    tables = ((jnp.cos(ang), cos_ref, False), (jnp.sin(ang), sin_ref, True))
    for k in range(ROPE_GROUPS):
        in_group = (lane_d // half) == k
        for table, ref, signed in tables:
            one = jnp.where(in_group, table, 0.0)
            full = one + pltpu.roll(one, half, 1)
            full = full + pltpu.roll(full, 2 * half, 1)
            if signed:
                full = jnp.where(first_half_d, -full, full)
            ref[pl.ds(k, tm // ROPE_GROUPS, stride=ROPE_GROUPS), :] = full
    cos = cos_ref[...]
    sin_signed = sin_ref[...]
    lane = lax.broadcasted_iota(jnp.int32, cos.shape, 1)
    first_half = (lane % HEAD_DIM) < half
    scale = HEAD_DIM ** -0.5 * LOG2E

    def proj(src_ref, lo, width):
        return jnp.dot(src_ref[...], win_ref[:, lo:lo + width], preferred_element_type=jnp.float32)

    offs = {}
    o = 0
    for name, width in (("aq", A_Q_W), ("ak", A_KV_W), ("av", A_KV_W), ("bq", B_W), ("bk", B_W), ("bv", B_W)):
        offs[name] = (o, width)
        o += width
    rope = lambda t: _rope(t, cos, sin_signed, first_half)

    def emit_bv():
        bv = proj(h2_next_ref, *offs["bv"])
        bv_ref[...] = bv.astype(bv_ref.dtype)
        _write_deinterleaved(bv, slab_ref, stage_ref, bv_d)

    def emit_bk():
        bk = rope(proj(h2_next_ref, *offs["bk"]))
        bk_ref[...] = bk.astype(bk_ref.dtype)
        _write_deinterleaved(bk, slab_ref, stage_ref, bk_d)

    def emit_bq():
        bq = rope(proj(h2_next_ref, *offs["bq"])) * scale
        bq_ref[...] = bq.astype(bq_ref.dtype)
        _write_deinterleaved(bq, slab_ref, stage_ref, bq_d)

    x = x_ref[...]
    h_ref[...] = _rms_norm(x, g1_ref[...]).astype(jnp.bfloat16)
    _swiglu_accumulate(h_ref, wg_ref, wu_ref, wd_ref, acc_ref, before_chunk={0: emit_bv, 3: emit_bk, 6: emit_bq})
    x1 = x + acc_ref[...]
    x1_ref[...] = x1
    h2_next_ref[...] = _rms_norm(x1, gm_ref[...]).astype(jnp.bfloat16)

    aq = jnp.dot(h2_cur_ref[...], waq_ref[...], preferred_element_type=jnp.float32)
    aq_ref[...] = (rope(aq) * scale).astype(aq_ref.dtype)
    akv = proj(h2_cur_ref, offs["ak"][0], A_KV_W + A_KV_W)
    ak_ref[...] = rope(akv[:, :A_KV_W]).astype(ak_ref.dtype)
    av_ref[...] = akv[:, A_KV_W:].astype(av_ref.dtype)


def _interleave_into(slab_ref, blk_ref, d, stage_ref=None):
    n_slabs = slab_ref.shape[0]
    rows = blk_ref.shape[1]
    if stage_ref is None:
        for r in range(d):
            for c in range(n_slabs):
                slab_ref[c, pl.ds(r, rows, stride=d), :] = blk_ref[r, :, c * LANES:(c + 1) * LANES].astype(jnp.float32)
        return
    d4 = DILATIONS[0]
    ratio = d // d4
    for r in range(d4):
        for c in range(n_slabs):
            for q in range(ratio):
                part = blk_ref[q * d4 + r, :, c * LANES:(c + 1) * LANES].astype(jnp.float32)
                stage_ref[r * n_slabs + c, pl.ds(q, rows, stride=ratio), :] = part
            slab_ref[c, pl.ds(r, rows * ratio, stride=d4), :] = stage_ref[r * n_slabs + c]


def _out_ffn_kernel(x1_ref, a_ref, o1_ref, l1_ref, o4_ref, l4_ref, o16_ref, l16_ref,
                    wo_ref, g2_ref, wg_ref, wu_ref, wd_ref, gf_ref,
                    out_ref, h_ref, acc_ref, so4_ref, sl4_ref, so16_ref, sl16_ref, stage_o_ref, stage_l_ref,
                    *, final_norm):
    d4, d16 = DILATIONS
    _interleave_into(so4_ref, o4_ref, d4)
    _interleave_into(sl4_ref, l4_ref, d4)
    _interleave_into(so16_ref, o16_ref, d16, stage_o_ref)
    _interleave_into(sl16_ref, l16_ref, d16, stage_l_ref)

    def merged(c):
        sl = slice(c * LANES, (c + 1) * LANES)
        la, lb, lc = l1_ref[:, sl], sl4_ref[c], sl16_ref[c]
        m = jnp.maximum(jnp.maximum(la, lb), lc)
        wa, wb, wc = jnp.exp2(la - m), jnp.exp2(lb - m), jnp.exp2(lc - m)
        num = wa * o1_ref[:, sl].astype(jnp.float32) + wb * so4_ref[c] + wc * so16_ref[c]
        return (num / (wa + wb + wc)).astype(jnp.bfloat16)

    mix = jnp.dot(a_ref[...], wo_ref[:A_Q_W, :], preferred_element_type=jnp.float32)
    per_dot = MXU_DEPTH // LANES
    for c in range(0, B_W // LANES, per_dot):
        b = jnp.concatenate([merged(c + j) for j in range(per_dot)], axis=1)
        mix += jnp.dot(b, wo_ref[A_Q_W + c * LANES:A_Q_W + (c + per_dot) * LANES, :],
                       preferred_element_type=jnp.float32)
    x2 = x1_ref[...] + mix
    h_ref[...] = _rms_norm(x2, g2_ref[...]).astype(jnp.bfloat16)
    _swiglu_accumulate(h_ref, wg_ref, wu_ref, wd_ref, acc_ref)
    x3 = x2 + acc_ref[...]
    out_ref[...] = _rms_norm(x3, gf_ref[...]) if final_norm else x3


def _attn_kernel(*refs, seq_len, tq, halo, sinks, emit_lse, cast_scales):
    if sinks:
        sink_ref, refs = refs[0], refs[1:]
    q_ref, kp_ref, kc_ref, kn_ref, vp_ref, vc_ref, vn_ref = refs[:7]
    refs = refs[7:]
    cast_in, refs = refs[:len(cast_scales)], refs[len(cast_scales):]
    o_ref, refs = refs[0], refs[1:]
    if emit_lse:
        lse_ref, refs = refs[0], refs[1:]
    cast_out, refs = refs[:len(cast_scales)], refs[len(cast_scales):]
    kbuf, vbuf, bias_ref = refs

    for src, dst, scale in zip(cast_in, cast_out, cast_scales):
        dst[...] = (src[...] if scale == 1 else src[...] * scale).astype(dst.dtype)

    n_seqs, rows = q_ref.shape[:2]
    tk = tq + 2 * halo
    tiles = rows // tq
    tiles_in_seq = seq_len // tq
    kv_tiles = kbuf.shape[2] // LANES
    q_tiles = q_ref.shape[2] // LANES
    step = pl.program_id(2)

    @pl.when((pl.program_id(0) == 0) & (pl.program_id(1) == 0) & (step == 0))
    def _():
        row = lax.broadcasted_iota(jnp.int32, (tq, tk), 0)
        col = lax.broadcasted_iota(jnp.int32, (tq, tk), 1)
        band = jnp.abs(row + halo - col) <= halo
        bias_ref[0] = jnp.where(band & (col >= halo), 0.0, NEG_BIG)
        bias_ref[1] = jnp.where(band, 0.0, NEG_BIG)
        bias_ref[2] = jnp.where(band & (col < tq + halo), 0.0, NEG_BIG)

    kbuf[:, 0:halo, :] = kp_ref[...]
    kbuf[:, halo:halo + rows, :] = kc_ref[...]
    kbuf[:, halo + rows:, :] = kn_ref[...]
    vbuf[:, 0:halo, :] = vp_ref[...]
    vbuf[:, halo:halo + rows, :] = vc_ref[...]
    vbuf[:, halo + rows:, :] = vn_ref[...]

    lane = lax.broadcasted_iota(jnp.int32, (tq, LANES), 1)
    low = lane < HEAD_DIM
    ones = jnp.ones((tk, LANES), jnp.bfloat16)
    zero_q = jnp.zeros((tq, LANES), jnp.bfloat16)

    def scores(n, t, c):
        kc = c if kv_tiles == q_tiles else 0
        r0 = t * tq
        qp = q_ref[n, r0:r0 + tq, c * LANES:(c + 1) * LANES]
        q2 = jnp.concatenate([jnp.where(low, qp, zero_q), jnp.where(low, zero_q, qp)], axis=0)
        kp = kbuf[n, r0:r0 + tk, kc * LANES:(kc + 1) * LANES]
        return lax.dot_general(q2, kp, (((1,), (1,)), ((), ())), preferred_element_type=jnp.float32)

    def finish(n, t, c, s):
        kc = c if kv_tiles == q_tiles else 0
        r0 = t * tq
        g = step * tiles + t
        bias = bias_ref[jnp.where(g == 0, 0, jnp.where(g == tiles_in_seq - 1, 2, 1))]
        vext = jnp.concatenate([vbuf[n, r0:r0 + tk, kc * LANES:(kc + 1) * LANES], ones], axis=1)
        s_lo = s[:tq] + bias
        s_hi = s[tq:] + bias
        m_lo = jnp.max(s_lo, axis=-1, keepdims=True)
        m_hi = jnp.max(s_hi, axis=-1, keepdims=True)
        if sinks:
            sink_lo = sink_ref[A_HEAD_ORDER[2 * c]] * LOG2E
            sink_hi = sink_ref[A_HEAD_ORDER[2 * c + 1]] * LOG2E
            m_lo = jnp.maximum(m_lo, sink_lo)
            m_hi = jnp.maximum(m_hi, sink_hi)
        p = jnp.concatenate([jnp.exp2(s_lo - m_lo), jnp.exp2(s_hi - m_hi)], axis=0).astype(jnp.bfloat16)
        o2 = jnp.dot(p, vext, preferred_element_type=jnp.float32)
        num = jnp.where(low, o2[:tq, :LANES], o2[tq:, :LANES])
        den = jnp.where(low, o2[:tq, LANES:], o2[tq:, LANES:])
        m = jnp.where(low, m_lo, m_hi)
        if sinks:
            den = den + jnp.exp2(jnp.where(low, sink_lo, sink_hi) - m)
        o_ref[n, r0:r0 + tq, c * LANES:(c + 1) * LANES] = (num / den).astype(o_ref.dtype)
        if emit_lse:
            lse_ref[n, r0:r0 + tq, c * LANES:(c + 1) * LANES] = m + jnp.log(den) * LOG2E

    for n, t, c in [(n, t, c) for n in range(n_seqs) for t in range(tiles) for c in range(q_tiles)]:
        finish(n, t, c, scores(n, t, c))


def _const_spec(shape):
    zeros = (0,) * len(shape)
    return pl.BlockSpec(shape, lambda *_: zeros, pipeline_mode=pl.Buffered(1))


def _ffn_in_proj(x3d, pos3d, freq, g1, ffn_w, gm, w_in):
    batch, seq_len, d = x3d.shape
    tm = TOKEN_TILE
    tiles = seq_len // tm
    bf = jnp.bfloat16
    n_tiles = batch * tiles
    ffn_tile = lambda t: jnp.minimum(t, n_tiles - 1)
    proj_tile = lambda t: jnp.maximum(t - 1, 0)
    row = lambda w, which: pl.BlockSpec((None, tm, w), lambda t: (which(t) // tiles, which(t) % tiles, 0))
    dil = lambda dd: pl.BlockSpec((None, dd, tm // dd, B_W),
                                  lambda t: (proj_tile(t) // tiles, 0, proj_tile(t) % tiles, 0))
    nat = lambda w, dt: jax.ShapeDtypeStruct((batch, seq_len, w), dt)
    dil_shape = lambda dd: jax.ShapeDtypeStruct((batch, dd, seq_len // dd, B_W), bf)
    out_specs = [row(d, ffn_tile)] + [row(w, proj_tile) for w in (A_Q_W, A_KV_W, A_KV_W, B_W, B_W, B_W)]
    out_shape = [nat(d, jnp.float32), nat(A_Q_W, bf), nat(A_KV_W, bf), nat(A_KV_W, bf),
                 nat(B_W, bf), nat(B_W, bf), nat(B_W, bf)]
    for _ in range(3):
        out_specs += [dil(dd) for dd in DILATIONS]
        out_shape += [dil_shape(dd) for dd in DILATIONS]
    return pl.pallas_call(
        _ffn_in_proj_kernel,
        grid=(n_tiles + 1,),
        in_specs=[row(d, ffn_tile),
                  pl.BlockSpec((None, tm // ROPE_GROUPS, LANES),
                               lambda t: (proj_tile(t) // tiles, proj_tile(t) % tiles, 0)),
                  _const_spec(freq.shape), _const_spec(g1.shape),
                  *[_const_spec(w.shape) for w in ffn_w], _const_spec(gm.shape), _const_spec(w_in.shape)],
        out_specs=out_specs,
        out_shape=out_shape,
        scratch_shapes=[pltpu.VMEM((tm, d), jnp.bfloat16), pltpu.VMEM((tm, d), jnp.bfloat16),
                        pltpu.VMEM((tm, d), jnp.bfloat16), pltpu.VMEM((tm, d), jnp.float32),
                        pltpu.VMEM((B_W // LANES, tm, LANES), jnp.float32),
                        pltpu.VMEM((DILATIONS[0] * B_W // LANES, tm // DILATIONS[0], LANES), jnp.float32),
                        pltpu.VMEM((tm, LANES), jnp.float32), pltpu.VMEM((tm, LANES), jnp.float32),
                        pltpu.VMEM((d, A_Q_W), jnp.bfloat16)],
        compiler_params=pltpu.CompilerParams(dimension_semantics=("arbitrary",),
                                             vmem_limit_bytes=VMEM_LIMIT),
        name="ffn_in_proj",
    )(x3d, pos3d, freq, g1, *ffn_w, gm, w_in)


def _out_ffn(x1, a, b_parts, wo, g2, ffn_w, gf, final_norm):
    batch, seq_len, d = x1.shape
    tm = TOKEN_TILE
    tiles = seq_len // tm
    row = lambda w: pl.BlockSpec((None, tm, w), lambda b, i: (b, i, 0))
    dil = lambda dd: pl.BlockSpec((None, dd, tm // dd, B_W), lambda b, i: (b, 0, i, 0))
    (o1, l1), (o4, l4), (o16, l16) = b_parts
    d4, d16 = DILATIONS
    slab = lambda: pltpu.VMEM((B_W // LANES, tm, LANES), jnp.float32)
    stage = lambda: pltpu.VMEM((d4 * B_W // LANES, tm // d4, LANES), jnp.float32)
    return pl.pallas_call(
        functools.partial(_out_ffn_kernel, final_norm=final_norm),
        grid=(batch, tiles),
        in_specs=[row(d), row(A_Q_W), row(B_W), row(B_W), dil(d4), dil(d4), dil(d16), dil(d16),
                  _const_spec(wo.shape), _const_spec(g2.shape),
                  *[_const_spec(w.shape) for w in ffn_w], _const_spec(gf.shape)],
        out_specs=row(d),
        out_shape=jax.ShapeDtypeStruct((batch, seq_len, d), jnp.float32),
        scratch_shapes=[pltpu.VMEM((tm, d), jnp.bfloat16), pltpu.VMEM((tm, d), jnp.float32),
                        slab(), slab(), slab(), slab(), stage(), stage()],
        compiler_params=pltpu.CompilerParams(dimension_semantics=("arbitrary", "arbitrary"),
                                             vmem_limit_bytes=VMEM_LIMIT),
        name="out_ffn",
    )(x1, a, o1, l1, o4, l4, o16, l16, wo, g2, *ffn_w, gf)


def _banded_attention(q, k, v, sink, *, tq, halo, step_rows, emit_lse, name, casts=()):
    batch, groups, seq_len, wq = q.shape
    wkv = k.shape[-1]
    rows = min(seq_len, step_rows)
    n_seqs = step_rows // rows
    steps = seq_len // rows
    ratio = rows // halo
    halo_blocks = seq_len // halo
    tile = lambda w: pl.BlockSpec((None, n_seqs, rows, w), lambda b, r, i: (b, r, i, 0))
    prev = pl.BlockSpec((None, n_seqs, halo, wkv), lambda b, r, i: (b, r, jnp.maximum(i * ratio - 1, 0), 0))
    nxt = pl.BlockSpec((None, n_seqs, halo, wkv),
                       lambda b, r, i: (b, r, jnp.minimum((i + 1) * ratio, halo_blocks - 1), 0))
    kv_specs = [prev, tile(wkv), nxt]
    in_specs = [tile(wq), *kv_specs, *kv_specs]
    args = [q, k, k, k, v, v, v]
    if sink is not None:
        in_specs = [pl.BlockSpec(memory_space=pltpu.SMEM)] + in_specs
        args = [sink] + args
    out_specs = [tile(wq)]
    out_shape = [jax.ShapeDtypeStruct(q.shape, jnp.bfloat16)]
    if emit_lse:
        out_specs.append(tile(wq))
        out_shape.append(jax.ShapeDtypeStruct(q.shape, jnp.float32))
    grid = (batch, groups // n_seqs, steps)
    n_steps = grid[0] * grid[1] * grid[2]
    flat_step = lambda b, r, i: (b * grid[1] + r) * grid[2] + i
    for w, _, src_block in casts:
        block = (w.shape[0] // n_steps, w.shape[1])
        in_specs.append(pl.BlockSpec(block, lambda b, r, i, f=src_block: (f(flat_step(b, r, i)), 0)))
        args.append(w)
        out_specs.append(pl.BlockSpec(block, lambda b, r, i: (flat_step(b, r, i), 0)))
        out_shape.append(jax.ShapeDtypeStruct(w.shape, jnp.bfloat16))
    tk = tq + 2 * halo
    res = pl.pallas_call(
        functools.partial(_attn_kernel, seq_len=seq_len, tq=tq, halo=halo, sinks=sink is not None,
                          emit_lse=emit_lse, cast_scales=tuple(scale for _, scale, _ in casts)),
        grid=grid,
        in_specs=in_specs,
        out_specs=out_specs,
        out_shape=out_shape,
        scratch_shapes=[pltpu.VMEM((n_seqs, rows + 2 * halo, wkv), jnp.bfloat16),
                        pltpu.VMEM((n_seqs, rows + 2 * halo, wkv), jnp.bfloat16),
                        pltpu.VMEM((3, tq, tk), jnp.float32)],
        compiler_params=pltpu.CompilerParams(dimension_semantics=("arbitrary",) * 3,
                                             vmem_limit_bytes=VMEM_LIMIT),
        name=name,
    )(*args)
    return res


def kernel(x, positions, norm_ffn1, w_gate1, w_up1, w_down1, norm_mix, w_in, a_sink, w_out,
           norm_ffn2, w_gate2, w_up2, w_down2, norm_final):
    batch, seq_len, d = x.shape
    depth = norm_ffn1.shape[0]
    bf = jnp.bfloat16

    inv_freq = 1.0 / (ROPE_THETA ** (jnp.arange(0, HEAD_DIM, 2, dtype=jnp.float32) / HEAD_DIM))
    freq = jnp.tile(inv_freq, LANES // (HEAD_DIM // 2))[None, :]
    pos3d = jnp.repeat(positions.reshape(batch, seq_len // ROPE_GROUPS, ROPE_GROUPS), HEAD_DIM // 2, axis=-1)
    xc = x
    same_block = lambda s: s
    per_group = A_Q_HEADS // A_KV_HEADS
    w_out_block = lambda s: jnp.where(s < A_Q_HEADS, (s % A_KV_HEADS) * per_group + s // A_KV_HEADS, s)

    for l in range(depth):
        ffn1 = (w_gate1[l].astype(bf), w_up1[l].astype(bf), (w_down1[l] * FFN_RES_WEIGHT).astype(bf))

        outs = _ffn_in_proj(xc, pos3d, freq, norm_ffn1[l][None, :], ffn1, norm_mix[l][None, :], w_in[l].astype(bf))
        x1, aq, ak, av, bq, bk, bv = outs[:7]
        n_dil = len(DILATIONS)
        bq_d, bk_d, bv_d = outs[7:7 + n_dil], outs[7 + n_dil:7 + 2 * n_dil], outs[7 + 2 * n_dil:]

        a_out, *ffn2, wo = _banded_attention(
            aq[:, None], ak[:, None], av[:, None], a_sink[l],
            tq=2 * A_HALF_WINDOW, halo=A_HALF_WINDOW, step_rows=ATTN_STEP_ROWS, emit_lse=False, name="attn_a",
            casts=((w_gate2[l], 1, same_block), (w_up2[l], 1, same_block),
                   (w_down2[l], FFN_RES_WEIGHT, same_block), (w_out[l], 1, w_out_block)))
        assert wo.shape[0] // (batch * (seq_len // ATTN_STEP_ROWS)) == HEAD_DIM
        b_parts = []
        for window, dilation in B_PATTERNS:
            if dilation == 1:
                qkv = (bq[:, None], bk[:, None], bv[:, None])
            else:
                j = DILATIONS.index(dilation)
                qkv = (bq_d[j], bk_d[j], bv_d[j])
            halo = window // (2 * dilation)
            o, lse = _banded_attention(*qkv, None, tq=2 * halo, halo=halo, step_rows=ATTN_STEP_ROWS,
                                       emit_lse=True, name=f"attn_b_d{dilation}")
            if dilation == 1:
                o, lse = o[:, 0], lse[:, 0]
            b_parts.append((o, lse))

        xc = _out_ffn(x1, a_out[:, 0], b_parts, wo, norm_ffn2[l][None, :], ffn2,
                      norm_final[None, :], final_norm=l == depth - 1)
    return xc
```
